```python
import jax, jax.numpy as jnp
from jax import lax
import numpy as np

D_MODEL = 2048
BATCH = 4
SEQ = 2048
DEPTH = 2
DEC_BATCH = 128
DEC_SEQ = 1
PAST_LEN = 16384
PAGE_SIZE = 128

D_MIX = D_MODEL
D_GROUP = D_MIX // 4
POOL_WINDOWS = (2, 4, 8, 16)
N_POOL_GROUPS = len(POOL_WINDOWS)
POOL_CH = D_GROUP // N_POOL_GROUPS
POOL_HIST = max(POOL_WINDOWS) - 1
MLA_HEADS = 4
MLA_NOPE = 128
MLA_ROPE = 32
MLA_V = D_GROUP // MLA_HEADS
MLA_Q_LORA = 384
MLA_KV_LORA = 128
ROPE_THETA = 10000.0
MLA_SCALE = (MLA_NOPE + MLA_ROPE) ** -0.5
CONV_WIDTH = 3
CONV_CH = D_GROUP
NSA_HEADS = 4
NSA_DK = D_GROUP // NSA_HEADS
NSA_L_CMP = 32
NSA_L_SEL = 64
NSA_TOPN = 16
NSA_WINDOW = 512
NSA_SCALE = NSA_DK ** -0.5
FORCED_SCORE = 1e6
Q_BLOCK = 128
RMS_EPS = 1e-6
NEG_INF = -1e30

IN_SPLITS = (
    ("pool_u", D_GROUP), ("pool_gate", D_GROUP),
    ("mla_cq", MLA_Q_LORA), ("mla_ckv", MLA_KV_LORA), ("mla_krope", MLA_ROPE), ("mla_gate", D_GROUP),
    ("conv_h", CONV_CH), ("conv_c", CONV_CH), ("conv_b", CONV_CH), ("conv_gate", CONV_CH),
    ("nsa_q", D_GROUP), ("nsa_kc", NSA_DK), ("nsa_vc", NSA_DK), ("nsa_ks", NSA_DK), ("nsa_vs", NSA_DK),
    ("nsa_kw", NSA_DK), ("nsa_vw", NSA_DK), ("nsa_branch_gate", 3 * NSA_HEADS), ("nsa_gate", D_GROUP),
)
D_IN = sum(w for _, w in IN_SPLITS)

kernel_name = "hybrid_pool_mla_conv_nsa_decode_step"

F32 = jnp.float32


def rmsnorm(x, g):
    xf = x.astype(F32)
    y = xf * lax.rsqrt(jnp.mean(xf * xf, axis=-1, keepdims=True) + RMS_EPS)
    return (y * g.astype(F32)).astype(x.dtype)


def split_proj(z):
    out, off = {}, 0
    for name, w in IN_SPLITS:
        out[name] = z[..., off:off + w]
        off += w
    return out


def alibi_slopes(n):
    return jnp.asarray([2.0 ** (-8.0 * (h + 1) / n) for h in range(n)], F32)


def rope(x, pos):
    half = x.shape[-1] // 2
    inv = ROPE_THETA ** (-jnp.arange(half, dtype=F32) / half)
    ang = pos.astype(F32)[:, None] * inv
    ang = ang.reshape((pos.shape[0],) + (1,) * (x.ndim - 3) + (half,))
    cos, sin = jnp.cos(ang), jnp.sin(ang)
    xf = x.astype(F32)
    x1, x2 = xf[..., :half], xf[..., half:]
    return jnp.concatenate([x1 * cos - x2 * sin, x1 * sin + x2 * cos], axis=-1).astype(x.dtype)


def masked_probs(scores, mask):
    s = jnp.where(mask, scores, NEG_INF)
    m = jnp.max(s, axis=-1, keepdims=True)
    e = jnp.where(mask, jnp.exp(s - m), 0.0)
    return e / jnp.maximum(jnp.sum(e, axis=-1, keepdims=True), 1e-30)


def map_query_blocks(fn, arrays, pos):
    B, T = arrays[0].shape[:2]
    nb = T // Q_BLOCK
    def to_blocks(a):
        return jnp.moveaxis(a.reshape((B, nb, Q_BLOCK) + a.shape[2:]), 1, 0)
    xs = tuple(to_blocks(a) for a in arrays) + (pos.reshape(nb, Q_BLOCK),)
    out = jnp.moveaxis(lax.map(lambda args: fn(*args), xs), 0, 1)
    return out.reshape((B, T) + out.shape[3:])


def pool_mix(u, prev, pos, w_pool, scale):
    B, T, C = u.shape
    full = jnp.concatenate([prev.astype(u.dtype), u], axis=1)
    cs = jnp.cumsum(full.astype(F32), axis=1)
    cs = jnp.concatenate([jnp.zeros((B, 1, C), F32), cs], axis=1)
    means = []
    for g, w in enumerate(POOL_WINDOWS):
        ch = slice(g * POOL_CH, (g + 1) * POOL_CH)
        hi = cs[:, POOL_HIST + 1:POOL_HIST + 1 + T, ch]
        lo = cs[:, POOL_HIST + 1 - w:POOL_HIST + 1 - w + T, ch]
        cnt = jnp.minimum(w, pos + 1).astype(F32)[None, :, None]
        means.append((hi - lo) / cnt)
    d = (jnp.concatenate(means, axis=-1) - u.astype(F32)).reshape(B, T, N_POOL_GROUPS, POOL_CH)
    y = jnp.einsum("btgc,gcd->btgd", d, w_pool.astype(F32)).reshape(B, T, C) * scale.astype(F32)
    return y, full[:, -POOL_HIST:]


def mla_queries(s, lw, pos):
    B, T = s["mla_cq"].shape[:2]
    cq = rmsnorm(s["mla_cq"], lw["mla_q_norm"])
    q = jnp.einsum("btc,cf->btf", cq, lw["mla_w_uq"]).reshape(B, T, MLA_HEADS, MLA_NOPE + MLA_ROPE)
    q_rope = rope(q[..., MLA_NOPE:], pos).astype(F32)
    q_lat = jnp.einsum("bthd,chd->bthc", q[..., :MLA_NOPE].astype(F32), lw["mla_w_uk"].astype(F32))
    return q_lat, q_rope


def mla_keys(s, lw, pos):
    latent = rmsnorm(s["mla_ckv"], lw["mla_kv_norm"])
    krope = rope(s["mla_krope"], pos)
    return latent, krope


def mla_attend(q_lat, q_rope, latent, krope, qpos, kpos, w_uv):
    B, T = q_lat.shape[:2]
    lat = latent.astype(F32)
    scores = (jnp.einsum("bthc,bsc->bths", q_lat, lat)
              + jnp.einsum("bthr,bsr->bths", q_rope, krope.astype(F32))) * MLA_SCALE
    mask = (kpos[None, :] <= qpos[:, None])[None, :, None, :]
    p = masked_probs(scores, mask)
    o_lat = jnp.einsum("bths,bsc->bthc", p, lat)
    o = jnp.einsum("bthc,chd->bthd", o_lat, w_uv.astype(F32))
    return o.reshape(B, T, MLA_HEADS * MLA_V)


def conv_mix(s, prev, w, b):
    z = s["conv_c"] * s["conv_h"]
    full = jnp.concatenate([prev.astype(z.dtype), z], axis=1)
    y = lax.conv_general_dilated(full, w[:, None, :].astype(full.dtype), window_strides=(1,),
                                 padding="VALID", dimension_numbers=("NWC", "WIO", "NWC"),
                                 feature_group_count=CONV_CH) + b
    return s["conv_b"] * y, full[:, -(CONV_WIDTH - 1):]


def nsa_compress(k, pos_gain):
    B, S, d = k.shape
    nb = S // NSA_L_CMP
    blocks = k[:, :nb * NSA_L_CMP].astype(F32).reshape(B, nb, NSA_L_CMP, d)
    return jnp.mean(blocks * pos_gain.astype(F32), axis=2)


def nsa_cmp_attend(q, kc, vc, qpos, slopes):
    nb = kc.shape[1]
    bend = (jnp.arange(nb) + 1) * NSA_L_CMP - 1
    dist = (qpos[:, None] - bend[None, :]).astype(F32)
    scores = jnp.einsum("bthd,bnd->bthn", q.astype(F32), kc) * NSA_SCALE - slopes[None, :, None] * dist[:, None, :]
    mask = (bend[None, :] <= qpos[:, None])[None, :, None, :]
    p = masked_probs(scores, mask)
    return jnp.einsum("bthn,bnd->bthd", p, vc), p


def nsa_select(p_cmp, qpos, n_keys):
    B, T, H, nc = p_cmp.shape
    ratio = NSA_L_SEL // NSA_L_CMP
    n_sb = -(-n_keys // NSA_L_SEL)
    imp = jnp.pad(p_cmp.sum(axis=2), ((0, 0), (0, 0), (0, n_sb * ratio - nc)))
    imp = imp.reshape(B, T, n_sb, ratio).sum(-1)
    blk = jnp.arange(n_sb)[None, :]
    cur = (qpos // NSA_L_SEL)[:, None]
    valid = blk * NSA_L_SEL <= qpos[:, None]
    forced = (blk == 0) | (blk == cur) | (blk == cur - 1)
    score = jnp.where(valid, jnp.where(forced, FORCED_SCORE, imp), -FORCED_SCORE)
    _, idx = lax.top_k(score, min(NSA_TOPN, n_sb))
    return idx


def nsa_sel_attend(q, idx, qpos, fetch, slopes):
    B, T, H, d = q.shape
    kpos = (idx[..., None] * NSA_L_SEL + jnp.arange(NSA_L_SEL)).reshape(B, T, -1)
    ks, vs = fetch(kpos)
    dist = (qpos[None, :, None] - kpos).astype(F32)
    scores = (jnp.einsum("bthd,btnd->bthn", q.astype(F32), ks.astype(F32)) * NSA_SCALE
              - slopes[None, None, :, None] * dist[:, :, None, :])
    p = masked_probs(scores, (dist >= 0)[:, :, None, :])
    return jnp.einsum("bthn,btnd->bthd", p, vs.astype(F32))


def nsa_win_prompt(q, kw, vw, pos, slopes):
    B, T, H, d = q.shape
    nb = T // Q_BLOCK
    span = NSA_WINDOW + Q_BLOCK
    kpad = jnp.pad(kw, ((0, 0), (NSA_WINDOW, 0), (0, 0)))
    vpad = jnp.pad(vw, ((0, 0), (NSA_WINDOW, 0), (0, 0)))
    kidx = (jnp.arange(nb) * Q_BLOCK)[:, None] + jnp.arange(span)[None, :]
    kb = kpad[:, kidx].astype(F32)
    vb = vpad[:, kidx].astype(F32)
    kpos = kidx - NSA_WINDOW
    qb = q.reshape(B, nb, Q_BLOCK, H, d).astype(F32)
    dist = pos.reshape(nb, Q_BLOCK)[:, :, None] - kpos[:, None, :]
    mask = ((dist >= 0) & (dist <= NSA_WINDOW) & (kpos[:, None, :] >= 0))[None, :, :, None, :]
    scores = (jnp.einsum("bnqhd,bnsd->bnqhs", qb, kb) * NSA_SCALE
              - slopes[:, None] * dist[:, :, None, :].astype(F32))
    p = masked_probs(scores, mask)
    return jnp.einsum("bnqhs,bnsd->bnqhd", p, vb).reshape(B, T, H, d)


def nsa_win_sample(q, buf_k, buf_v, k_new, v_new, pos, slopes):
    wb = buf_k.shape[1]
    T = k_new.shape[1]
    k = jnp.concatenate([buf_k, k_new.astype(buf_k.dtype)], axis=1)
    v = jnp.concatenate([buf_v, v_new.astype(buf_v.dtype)], axis=1)
    kpos = pos[0] - wb + jnp.arange(wb + T)
    dist = pos[:, None] - kpos[None, :]
    scores = (jnp.einsum("bthd,bsd->bths", q.astype(F32), k.astype(F32)) * NSA_SCALE
              - slopes[None, :, None] * dist[:, None, :].astype(F32))
    mask = ((dist >= 0) & (dist <= NSA_WINDOW))[None, :, None, :]
    p = masked_probs(scores, mask)
    return jnp.einsum("bths,bsd->bthd", p, v.astype(F32)), k[:, -wb:], v[:, -wb:]


def nsa_combine(gate_logits, o_cmp, o_sel, o_win):
    B, T = gate_logits.shape[:2]
    g = jax.nn.sigmoid(gate_logits.astype(F32)).reshape(B, T, 3, NSA_HEADS, 1)
    o = g[:, :, 0] * o_cmp + g[:, :, 1] * o_sel + g[:, :, 2] * o_win
    return o.reshape(B, T, NSA_HEADS * NSA_DK)


def mixer_output(x, s, yA, yB, yC, yD, w_out):
    parts = [y.astype(x.dtype) * jax.nn.silu(s[g]) for y, g in
             zip((yA, yB, yC, yD), ("pool_gate", "mla_gate", "conv_gate", "nsa_gate"))]
    return x + jnp.einsum("btf,fd->btd", jnp.concatenate(parts, axis=-1), w_out)


def layer_prompt(x, lw, pos):
    B, T, _ = x.shape
    s = split_proj(jnp.einsum("btd,df->btf", rmsnorm(x, lw["norm"]), lw["w_in"]))
    yA, pool_state = pool_mix(s["pool_u"], jnp.zeros((B, POOL_HIST, D_GROUP), x.dtype), pos,
                              lw["pool_w"], lw["pool_scale"])
    q_lat, q_rope = mla_queries(s, lw, pos)
    latent, krope = mla_keys(s, lw, pos)
    yB = map_query_blocks(lambda ql, qr, qp: mla_attend(ql, qr, latent, krope, qp, pos, lw["mla_w_uv"]),
                          (q_lat, q_rope), pos)
    yC, conv_state = conv_mix(s, jnp.zeros((B, CONV_WIDTH - 1, CONV_CH), x.dtype), lw["conv_w"], lw["conv_b"])
    slopes = alibi_slopes(NSA_HEADS)
    q = s["nsa_q"].reshape(B, T, NSA_HEADS, NSA_DK)
    kc = nsa_compress(s["nsa_kc"], lw["nsa_cmp_pos_k"])
    vc = nsa_compress(s["nsa_vc"], lw["nsa_cmp_pos_v"])
    o_cmp, p_cmp = nsa_cmp_attend(q, kc, vc, pos, slopes)
    sel_idx = nsa_select(p_cmp, pos, T)
    ks_all, vs_all = s["nsa_ks"], s["nsa_vs"]
    bidx = jnp.arange(B)[:, None, None]
    def fetch(kp):
        kp = jnp.clip(kp, 0, T - 1)
        return ks_all[bidx, kp], vs_all[bidx, kp]
    o_sel = map_query_blocks(lambda qb, ib, pb: nsa_sel_attend(qb, ib, pb, fetch, slopes), (q, sel_idx), pos)
    o_win = nsa_win_prompt(q, s["nsa_kw"], s["nsa_vw"], pos, slopes)
    yD = nsa_combine(s["nsa_branch_gate"], o_cmp, o_sel, o_win)
    y = mixer_output(x, s, yA, yB, yC, yD, lw["w_out"])
    wbp = min(NSA_WINDOW, T)
    new = (latent, krope, s["nsa_kc"], s["nsa_vc"], ks_all, vs_all,
           s["nsa_kw"][:, T - wbp:], s["nsa_vw"][:, T - wbp:], conv_state, pool_state)
    return y, new


def gather_pages(cache, l, page_table):
    g = cache[l, page_table]
    return g.reshape(g.shape[0], -1, g.shape[-1])


def layer_sample(x, lw, c, l, page_table, pos):
    B, T, _ = x.shape
    past = page_table.shape[1] * PAGE_SIZE
    s = split_proj(jnp.einsum("btd,df->btf", rmsnorm(x, lw["norm"]), lw["w_in"]))
    yA, pool_state = pool_mix(s["pool_u"], c["pool"][l], pos, lw["pool_w"], lw["pool_scale"])
    q_lat, q_rope = mla_queries(s, lw, pos)
    latent, krope = mla_keys(s, lw, pos)
    lat_all = jnp.concatenate([gather_pages(c["mla_latent"], l, page_table),
                               latent.astype(c["mla_latent"].dtype)], axis=1)
    kr_all = jnp.concatenate([gather_pages(c["mla_krope"], l, page_table),
                              krope.astype(c["mla_krope"].dtype)], axis=1)
    yB = mla_attend(q_lat, q_rope, lat_all, kr_all, pos, jnp.arange(past + T), lw["mla_w_uv"])
    yC, conv_state = conv_mix(s, c["conv"][l], lw["conv_w"], lw["conv_b"])
    slopes = alibi_slopes(NSA_HEADS)
    q = s["nsa_q"].reshape(B, T, NSA_HEADS, NSA_DK)
    kc_all = jnp.concatenate([gather_pages(c["cmp_k"], l, page_table),
                              s["nsa_kc"].astype(c["cmp_k"].dtype)], axis=1)
    vc_all = jnp.concatenate([gather_pages(c["cmp_v"], l, page_table),
                              s["nsa_vc"].astype(c["cmp_v"].dtype)], axis=1)
    kc = nsa_compress(kc_all, lw["nsa_cmp_pos_k"])
    vc = nsa_compress(vc_all, lw["nsa_cmp_pos_v"])
    o_cmp, p_cmp = nsa_cmp_attend(q, kc, vc, pos, slopes)
    sel_idx = nsa_select(p_cmp, pos, past + T)
    bidx = jnp.arange(B)[:, None, None]
    def fetch(kp):
        in_past = (kp < past)[..., None]
        pp = jnp.clip(kp, 0, past - 1)
        phys = page_table[bidx, pp // PAGE_SIZE]
        off = pp % PAGE_SIZE
        jn = jnp.clip(kp - past, 0, T - 1)
        k = jnp.where(in_past, c["sel_k"][l, phys, off], s["nsa_ks"][bidx, jn].astype(c["sel_k"].dtype))
        v = jnp.where(in_past, c["sel_v"][l, phys, off], s["nsa_vs"][bidx, jn].astype(c["sel_v"].dtype))
        return k, v
    o_sel = nsa_sel_attend(q, sel_idx, pos, fetch, slopes)
    o_win, win_k, win_v = nsa_win_sample(q, c["win_k"][l], c["win_v"][l], s["nsa_kw"], s["nsa_vw"], pos, slopes)
    yD = nsa_combine(s["nsa_branch_gate"], o_cmp, o_sel, o_win)
    y = mixer_output(x, s, yA, yB, yC, yD, lw["w_out"])
    new = (latent, krope, s["nsa_kc"], s["nsa_vc"], s["nsa_ks"], s["nsa_vs"],
           win_k, win_v, conv_state, pool_state)
    return y, new


def stack_layers(rows):
    return tuple(jnp.stack([r[i] for r in rows]) for i in range(len(rows[0])))


def setup_inputs(seed: int = 0) -> dict:
    key = jax.random.key(seed)
    ks = jax.random.split(key, 32)
    def nrm(i, shape, scale):
        return jax.random.normal(ks[i], shape, F32) * scale
    n_pages = PAST_LEN // PAGE_SIZE
    n_phys = (DEC_BATCH * n_pages * 5) // 4
    wb = min(NSA_WINDOW, PAST_LEN)
    perm = jax.random.permutation(ks[31], n_phys)
    page_table = perm[:DEC_BATCH * n_pages].reshape(DEC_BATCH, n_pages).astype(jnp.int32)
    return {
        "x_prompt": nrm(0, (BATCH, SEQ, D_MODEL), 1.0),
        "x_sample": nrm(1, (DEC_BATCH, DEC_SEQ, D_MODEL), 1.0),
        "cache_mla_latent": nrm(2, (DEPTH, n_phys, PAGE_SIZE, MLA_KV_LORA), 1.0),
        "cache_mla_krope": nrm(3, (DEPTH, n_phys, PAGE_SIZE, MLA_ROPE), 1.0),
        "cache_nsa_cmp_k": nrm(4, (DEPTH, n_phys, PAGE_SIZE, NSA_DK), 1.0),
        "cache_nsa_cmp_v": nrm(5, (DEPTH, n_phys, PAGE_SIZE, NSA_DK), 1.0),
        "cache_nsa_sel_k": nrm(6, (DEPTH, n_phys, PAGE_SIZE, NSA_DK), 1.0),
        "cache_nsa_sel_v": nrm(7, (DEPTH, n_phys, PAGE_SIZE, NSA_DK), 1.0),
        "state_nsa_win_k": nrm(8, (DEPTH, DEC_BATCH, wb, NSA_DK), 1.0),
        "state_nsa_win_v": nrm(9, (DEPTH, DEC_BATCH, wb, NSA_DK), 1.0),
        "state_conv": nrm(10, (DEPTH, DEC_BATCH, CONV_WIDTH - 1, CONV_CH), 1.0),
        "state_pool": nrm(11, (DEPTH, DEC_BATCH, POOL_HIST, D_GROUP), 1.0),
        "page_table": page_table,
        "norm_gain": 1.0 + nrm(12, (DEPTH, D_MODEL), 0.02),
        "w_in": nrm(13, (DEPTH, D_MODEL, D_IN), D_MODEL ** -0.5),
        "w_out": nrm(14, (DEPTH, D_MIX, D_MODEL), D_MIX ** -0.5),
        "pool_w": nrm(15, (DEPTH, N_POOL_GROUPS, POOL_CH, POOL_CH), POOL_CH ** -0.5),
        "pool_scale": 1.0 + nrm(16, (DEPTH, D_GROUP), 0.1),
        "mla_q_norm": 1.0 + nrm(17, (DEPTH, MLA_Q_LORA), 0.02),
        "mla_kv_norm": 1.0 + nrm(18, (DEPTH, MLA_KV_LORA), 0.02),
        "mla_w_uq": nrm(19, (DEPTH, MLA_Q_LORA, MLA_HEADS * (MLA_NOPE + MLA_ROPE)), MLA_Q_LORA ** -0.5),
        "mla_w_uk": nrm(20, (DEPTH, MLA_KV_LORA, MLA_HEADS, MLA_NOPE), MLA_KV_LORA ** -0.5),
        "mla_w_uv": nrm(21, (DEPTH, MLA_KV_LORA, MLA_HEADS, MLA_V), MLA_KV_LORA ** -0.5),
        "conv_w": nrm(22, (DEPTH, CONV_WIDTH, CONV_CH), CONV_WIDTH ** -0.5),
        "conv_b": nrm(23, (DEPTH, CONV_CH), 0.01),
        "nsa_cmp_pos_k": 1.0 + nrm(24, (DEPTH, NSA_L_CMP, NSA_DK), 0.1),
        "nsa_cmp_pos_v": 1.0 + nrm(25, (DEPTH, NSA_L_CMP, NSA_DK), 0.1),
        "final_norm": 1.0 + nrm(26, (D_MODEL,), 0.02),
    }


def reference(x_prompt, x_sample, cache_mla_latent, cache_mla_krope, cache_nsa_cmp_k, cache_nsa_cmp_v,
              cache_nsa_sel_k, cache_nsa_sel_v, state_nsa_win_k, state_nsa_win_v, state_conv, state_pool,
              page_table, norm_gain, w_in, w_out, pool_w, pool_scale, mla_q_norm, mla_kv_norm,
              mla_w_uq, mla_w_uk, mla_w_uv, conv_w, conv_b, nsa_cmp_pos_k, nsa_cmp_pos_v, final_norm):
    pos_p = jnp.arange(x_prompt.shape[1], dtype=jnp.int32)
    pos_s = PAST_LEN + jnp.arange(x_sample.shape[1], dtype=jnp.int32)
    caches = {"mla_latent": cache_mla_latent, "mla_krope": cache_mla_krope,
              "cmp_k": cache_nsa_cmp_k, "cmp_v": cache_nsa_cmp_v,
              "sel_k": cache_nsa_sel_k, "sel_v": cache_nsa_sel_v,
              "win_k": state_nsa_win_k, "win_v": state_nsa_win_v,
              "conv": state_conv, "pool": state_pool}
    xp, xs = x_prompt, x_sample
    new_p, new_s = [], []
    for l in range(DEPTH):
        lw = {"norm": norm_gain[l], "w_in": w_in[l], "w_out": w_out[l], "pool_w": pool_w[l],
              "pool_scale": pool_scale[l], "mla_q_norm": mla_q_norm[l], "mla_kv_norm": mla_kv_norm[l],
              "mla_w_uq": mla_w_uq[l], "mla_w_uk": mla_w_uk[l], "mla_w_uv": mla_w_uv[l],
              "conv_w": conv_w[l], "conv_b": conv_b[l],
              "nsa_cmp_pos_k": nsa_cmp_pos_k[l], "nsa_cmp_pos_v": nsa_cmp_pos_v[l]}
        xp, rows_p = layer_prompt(xp, lw, pos_p)
        xs, rows_s = layer_sample(xs, lw, caches, l, page_table, pos_s)
        new_p.append(rows_p)
        new_s.append(rows_s)
    (p_mla_latent, p_mla_krope, p_cmp_k, p_cmp_v, p_sel_k, p_sel_v,
     p_win_k, p_win_v, p_conv, p_pool) = stack_layers(new_p)
    (s_mla_latent, s_mla_krope, s_cmp_k, s_cmp_v, s_sel_k, s_sel_v,
     s_win_k, s_win_v, s_conv, s_pool) = stack_layers(new_s)
    y_prompt = rmsnorm(xp, final_norm)
    y_sample = rmsnorm(xs, final_norm)
    return (y_prompt, y_sample,
            p_mla_latent, p_mla_krope, p_cmp_k, p_cmp_v, p_sel_k, p_sel_v, p_win_k, p_win_v, p_conv, p_pool,
            s_mla_latent, s_mla_krope, s_cmp_k, s_cmp_v, s_sel_k, s_sel_v, s_win_k, s_win_v, s_conv, s_pool)
```

```python
import functools

import numpy as np
import jax
import jax.numpy as jnp
from jax import lax
from jax.experimental import pallas as pl
from jax.experimental.pallas import tpu as pltpu

F32 = jnp.float32
BF16 = jnp.bfloat16

D_MODEL = 2048
D_GROUP = 512
POOL_WINDOWS = (2, 4, 8, 16)
POOL_CH = 128
POOL_HIST = 15
MLA_HEADS = 4
MLA_NOPE = 128
MLA_ROPE = 32
MLA_Q_LORA = 384
MLA_KV_LORA = 128
ROPE_THETA = 10000.0
MLA_SCALE = (MLA_NOPE + MLA_ROPE) ** -0.5
CONV_WIDTH = 3
NSA_HEADS = 4
NSA_DK = 128
NSA_L_CMP = 32
NSA_L_SEL = 64
NSA_TOPN = 16
NSA_WINDOW = 512
NSA_SCALE = NSA_DK ** -0.5
FORCED_SCORE = 1e6
RMS_EPS = 1e-6
NEG_INF = -1e30
PAGE_SIZE = 128
ALIBI = tuple(2.0 ** (-8.0 * (h + 1) / NSA_HEADS) for h in range(NSA_HEADS))

LANES = 128
Q_BLOCK = 128
K_BLOCK = 128
VMEM_LIMIT = 56 * 1024 * 1024

Z_POOL_U, Z_POOL_G = 0, 512
Z_CQ, Z_CKV = 1024, 1408
Z_MLA_G = 1536
Z_CONV_H, Z_CONV_C, Z_CONV_B, Z_CONV_G = 2048, 2560, 3072, 3584
Z_NSA_Q, Z_NSA_G = 4096, 4608
Z_KC, Z_VC, Z_KS, Z_VS, Z_KW, Z_VW = 5120, 5248, 5376, 5504, 5632, 5760
Z_MISC = 5888
Z_WIDTH = 6144
MISC_GATE0 = 32


def _nn(a, b):
    return jnp.dot(a, b, preferred_element_type=F32)


def _nt(a, b):
    return lax.dot_general(a, b, (((1,), (1,)), ((), ())), preferred_element_type=F32)


def _tn(a, b):
    return lax.dot_general(a, b, (((0,), (0,)), ((), ())), preferred_element_type=F32)


def _rms(x, g):
    return x * lax.rsqrt(jnp.mean(x * x, axis=-1, keepdims=True) + RMS_EPS) * g


def _params(n_axes):
    return pltpu.CompilerParams(dimension_semantics=("arbitrary",) * n_axes,
                                vmem_limit_bytes=VMEM_LIMIT)


def _inproj_kernel(x_ref, g_ref, w_ref, o_ref, xn_ref, *, tm):
    @pl.when(pl.program_id(1) == 0)
    def _():
        rows = 64 if tm % 64 == 0 else tm

        def body(i, c):
            r = pl.multiple_of(i * rows, rows)
            xn_ref[pl.ds(r, rows), :] = _rms(x_ref[pl.ds(r, rows), :], g_ref[...]).astype(BF16)
            return c
        lax.fori_loop(0, tm // rows, body, 0)

    o_ref[...] = _nn(xn_ref[...], w_ref[...])


def _inproj(x, g, w, *, tm, tn):
    m = x.shape[0]
    return pl.pallas_call(
        functools.partial(_inproj_kernel, tm=tm),
        grid=(m // tm, Z_WIDTH // tn),
        in_specs=[pl.BlockSpec((tm, D_MODEL), lambda i, j: (i, 0)),
                  pl.BlockSpec((1, D_MODEL), lambda i, j: (0, 0)),
                  pl.BlockSpec((D_MODEL, tn), lambda i, j: (0, j))],
        out_specs=pl.BlockSpec((tm, tn), lambda i, j: (i, j)),
        out_shape=jax.ShapeDtypeStruct((m, Z_WIDTH), F32),
        scratch_shapes=[pltpu.VMEM((tm, D_MODEL), BF16)],
        compiler_params=_params(2),
        name="inproj",
    )(x, g, w)


def _rope128(x, cos, sin, lane):
    sw = jnp.where((lane & (MLA_ROPE - 1)) < MLA_ROPE // 2, pltpu.roll(x, LANES - MLA_ROPE // 2, 1),
                   pltpu.roll(x, MLA_ROPE // 2, 1))
    return x * cos + sw * sin


def _prep_kernel(zq_ref, zn_ref, cos_ref, sin_ref, qg_ref, kg_ref, wqn_ref, wqr_ref, wuk_ref,
                 gk_ref, gv_ref, *out_refs, tb, with_nsa):
    if with_nsa:
        qcat_ref, kcat_ref, lat_ref, kr_ref, kvb_ref, kc_ref, vc_ref = out_refs
    else:
        qcat_ref, kcat_ref, lat_ref, kr_ref = out_refs
    zq = zq_ref[...]
    cos = cos_ref[...]
    sin = sin_ref[...]
    lane = lax.broadcasted_iota(jnp.int32, (tb, LANES), 1)
    cqn = _rms(zq[:, :MLA_Q_LORA], qg_ref[...]).astype(BF16)
    lat = _rms(zq[:, MLA_Q_LORA:], kg_ref[...])
    q_nope = _nn(cqn, wqn_ref[...])
    q_rope = _rope128(_nn(cqn, wqr_ref[...]), cos, sin, lane) * MLA_SCALE
    misc = zn_ref[:, Z_MISC - Z_KC:Z_MISC - Z_KC + LANES]
    kr = jnp.where(lane < MLA_ROPE, _rope128(misc, cos, sin, lane), 0.0)
    for h in range(MLA_HEADS):
        q_lat = _nn(q_nope[:, h * MLA_NOPE:(h + 1) * MLA_NOPE].astype(BF16), wuk_ref[h]) * MLA_SCALE
        qcat_ref[:, h * 256:h * 256 + LANES] = q_lat.astype(BF16)
        qr = q_rope if h == 0 else pltpu.roll(q_rope, LANES - MLA_ROPE * h, 1)
        qcat_ref[:, h * 256 + LANES:(h + 1) * 256] = jnp.where(lane < MLA_ROPE, qr, 0.0).astype(BF16)
    kcat_ref[:, :LANES] = lat.astype(BF16)
    kcat_ref[:, LANES:] = kr.astype(BF16)
    lat_ref[...] = lat
    kr_ref[...] = kr
    if with_nsa:
        kvb_ref[...] = zn_ref[:, Z_KS - Z_KC:Z_VW - Z_KC + LANES].astype(BF16)
        nb = tb // NSA_L_CMP
        kcb = zn_ref[:, 0:LANES].reshape(nb, NSA_L_CMP, LANES) * gk_ref[...][None]
        vcb = zn_ref[:, LANES:2 * LANES].reshape(nb, NSA_L_CMP, LANES) * gv_ref[...][None]
        kc_ref[...] = jnp.sum(kcb, axis=1) * (1.0 / NSA_L_CMP)
        vc_ref[...] = jnp.sum(vcb, axis=1) * (1.0 / NSA_L_CMP)


def _prep(z, cos, sin, lw, *, tb, rows_per_table, with_nsa):
    m = z.shape[0]
    nt = rows_per_table // tb
    row = lambda i: (i, 0)
    full2 = lambda i: (0, 0)
    in_specs = [pl.BlockSpec((tb, 512), lambda i: (i, Z_CQ // 512)),
                pl.BlockSpec((tb, 1024), lambda i: (i, Z_KC // 1024)),
                pl.BlockSpec((tb, LANES), lambda i: (i % nt, 0)),
                pl.BlockSpec((tb, LANES), lambda i: (i % nt, 0)),
                pl.BlockSpec((1, MLA_Q_LORA), full2),
                pl.BlockSpec((1, MLA_KV_LORA), full2),
                pl.BlockSpec((MLA_Q_LORA, MLA_HEADS * MLA_NOPE), full2),
                pl.BlockSpec((MLA_Q_LORA, MLA_HEADS * MLA_ROPE), full2),
                pl.BlockSpec((MLA_HEADS, MLA_NOPE, MLA_KV_LORA), lambda i: (0, 0, 0)),
                pl.BlockSpec((NSA_L_CMP, NSA_DK), full2),
                pl.BlockSpec((NSA_L_CMP, NSA_DK), full2)]
    out_specs = [pl.BlockSpec((tb, 1024), row), pl.BlockSpec((tb, 256), row),
                 pl.BlockSpec((tb, LANES), row), pl.BlockSpec((tb, LANES), row)]
    out_shape = [jax.ShapeDtypeStruct((m, 1024), BF16), jax.ShapeDtypeStruct((m, 256), BF16),
                 jax.ShapeDtypeStruct((m, LANES), F32), jax.ShapeDtypeStruct((m, LANES), F32)]
    if with_nsa:
        nb = tb // NSA_L_CMP
        out_specs += [pl.BlockSpec((tb, 512), row), pl.BlockSpec((nb, LANES), row), pl.BlockSpec((nb, LANES), row)]
        out_shape += [jax.ShapeDtypeStruct((m, 512), BF16),
                      jax.ShapeDtypeStruct((m // NSA_L_CMP, LANES), F32),
                      jax.ShapeDtypeStruct((m // NSA_L_CMP, LANES), F32)]
    return pl.pallas_call(
        functools.partial(_prep_kernel, tb=tb, with_nsa=with_nsa),
        grid=(m // tb,),
        in_specs=in_specs, out_specs=out_specs, out_shape=out_shape,
        compiler_params=_params(1),
        name="prep_nsa" if with_nsa else "prep",
    )(z, z, cos, sin, lw["q_norm"], lw["kv_norm"], lw["w_uq_nope"], lw["w_uq_rope"], lw["w_ukT"],
      lw["gain_k"], lw["gain_v"])


def _online_step(s, valid, v, m_ref, l_ref, acc_ref, h):
    s = jnp.where(valid, s, NEG_INF)
    m_prev = m_ref[h]
    m_new = jnp.maximum(m_prev, jnp.max(s, axis=1, keepdims=True))
    alpha = jnp.exp(m_prev - m_new)
    p = jnp.where(valid, jnp.exp(s - m_new), 0.0)
    l_ref[h] = alpha * l_ref[h] + jnp.sum(p, axis=1, keepdims=True)
    acc_ref[h] = alpha * acc_ref[h] + _nn(p.astype(BF16), v)
    m_ref[h] = m_new


def _reset_state(m_ref, l_ref, acc_ref):
    m_ref[...] = jnp.full(m_ref.shape, NEG_INF, F32)
    l_ref[...] = jnp.zeros(l_ref.shape, F32)
    acc_ref[...] = jnp.zeros(acc_ref.shape, F32)


def _mla_prompt_kernel(q_ref, k_ref, o_ref, m_ref, l_ref, acc_ref):
    qi = pl.program_id(1)
    _reset_state(m_ref, l_ref, acc_ref)
    rc = (lax.broadcasted_iota(jnp.int32, (Q_BLOCK, K_BLOCK), 0)
          - lax.broadcasted_iota(jnp.int32, (Q_BLOCK, K_BLOCK), 1))

    def body(kb, carry):
        k0 = pl.multiple_of(kb * K_BLOCK, K_BLOCK)
        k = k_ref[pl.ds(k0, K_BLOCK), :]
        v = k[:, :MLA_KV_LORA]
        valid = rc + (qi - kb) * K_BLOCK >= 0
        for h in range(MLA_HEADS):
            s = _nt(q_ref[:, h * 256:(h + 1) * 256], k)
            _online_step(s, valid, v, m_ref, l_ref, acc_ref, h)
        return carry

    lax.fori_loop(0, qi + 1, body, 0)
    for h in range(MLA_HEADS):
        o_ref[:, h * LANES:(h + 1) * LANES] = acc_ref[h] / jnp.maximum(l_ref[h], 1e-30)


def _mla_prompt(qcat, kcat, *, batch, seq):
    nq = seq // Q_BLOCK
    return pl.pallas_call(
        _mla_prompt_kernel,
        grid=(batch, nq),
        in_specs=[pl.BlockSpec((Q_BLOCK, 1024), lambda b, i: (b * nq + i, 0)),
                  pl.BlockSpec((seq, 256), lambda b, i: (b, 0))],
        out_specs=pl.BlockSpec((Q_BLOCK, 512), lambda b, i: (b * nq + i, 0)),
        out_shape=jax.ShapeDtypeStruct((batch * seq, 512), F32),
        scratch_shapes=[pltpu.VMEM((MLA_HEADS, Q_BLOCK, LANES), F32)] * 3,
        compiler_params=_params(2),
        name="mla_prompt",
    )(qcat, kcat)


def _nsa_prompt_kernel(q_ref, misc_ref, kc_ref, vc_ref, kvb_ref, e_ref, o_ref,
                       selmask_ref, m_ref, l_ref, acc_ref, *, seq, n_sel):
    qi = pl.program_id(1)
    t0 = qi * Q_BLOCK
    nc = seq // NSA_L_CMP
    q = q_ref[...] * NSA_SCALE
    qh = [q[:, h * NSA_DK:(h + 1) * NSA_DK].astype(BF16) for h in range(NSA_HEADS)]
    gates = jax.nn.sigmoid(misc_ref[...])

    def gate(j, h):
        c = MISC_GATE0 + NSA_HEADS * j + h
        return gates[:, c:c + 1]

    kc = kc_ref[...].astype(BF16)
    vc = vc_ref[...].astype(BF16)
    t_a = t0 + lax.broadcasted_iota(jnp.int32, (Q_BLOCK, nc), 0)
    bend_a = lax.broadcasted_iota(jnp.int32, (Q_BLOCK, nc), 1) * NSA_L_CMP + (NSA_L_CMP - 1)
    mask_a = bend_a <= t_a
    dist_a = (t_a - bend_a).astype(F32)
    t_b = t0 + lax.broadcasted_iota(jnp.int32, (nc, Q_BLOCK), 1)
    bend_b = lax.broadcasted_iota(jnp.int32, (nc, Q_BLOCK), 0) * NSA_L_CMP + (NSA_L_CMP - 1)
    mask_b = bend_b <= t_b
    dist_b = (t_b - bend_b).astype(F32)
    imp = jnp.zeros((nc, Q_BLOCK), F32)
    for h in range(NSA_HEADS):
        s_a = jnp.where(mask_a, _nt(qh[h], kc) - ALIBI[h] * dist_a, NEG_INF)
        e_a = jnp.where(mask_a, jnp.exp(s_a - jnp.max(s_a, axis=1, keepdims=True)), 0.0)
        p_a = e_a / jnp.maximum(jnp.sum(e_a, axis=1, keepdims=True), 1e-30)
        o_ref[:, h * NSA_DK:(h + 1) * NSA_DK] = gate(0, h) * _nn(p_a.astype(BF16), vc)
        s_b = jnp.where(mask_b, _nt(kc, qh[h]) - ALIBI[h] * dist_b, NEG_INF)
        e_b = jnp.where(mask_b, jnp.exp(s_b - jnp.max(s_b, axis=0, keepdims=True)), 0.0)
        imp = imp + e_b / jnp.maximum(jnp.sum(e_b, axis=0, keepdims=True), 1e-30)

    imp2 = imp + pltpu.roll(imp, nc - 1, 0)
    ridx = lax.broadcasted_iota(jnp.int32, (nc, Q_BLOCK), 0)
    blk = ridx >> 1
    cur = t_b >> 6
    forced = (blk == 0) | (blk == cur) | (blk == cur - 1)
    score = jnp.where(blk <= cur, jnp.where(forced, FORCED_SCORE, imp2), -FORCED_SCORE)
    rank = jnp.zeros((nc, Q_BLOCK), F32)
    for i in range(0, nc, 2):
        s_i = score[i:i + 1, :]
        ahead = (s_i > score) | ((s_i == score) & (ridx > i))
        rank = rank + jnp.where(ahead, 1.0, 0.0)
    sel_t = jnp.where((rank < n_sel) & ((ridx & 1) == 0), 1.0, 0.0).astype(BF16)
    selmask_ref[...] = _tn(sel_t, e_ref[...])

    rc = (lax.broadcasted_iota(jnp.int32, (Q_BLOCK, K_BLOCK), 0)
          - lax.broadcasted_iota(jnp.int32, (Q_BLOCK, K_BLOCK), 1))

    def branch(first_kb, col_k, col_v, use_sel):
        _reset_state(m_ref, l_ref, acc_ref)

        def body(kb, carry):
            k0 = pl.multiple_of(kb * K_BLOCK, K_BLOCK)
            k = kvb_ref[pl.ds(k0, K_BLOCK), col_k:col_k + NSA_DK]
            v = kvb_ref[pl.ds(k0, K_BLOCK), col_v:col_v + NSA_DK]
            dist_i = rc + (qi - kb) * K_BLOCK
            dist = dist_i.astype(F32)
            if use_sel:
                valid = (dist_i >= 0) & (selmask_ref[:, pl.ds(k0, K_BLOCK)] > 0.5)
            else:
                valid = (dist_i >= 0) & (dist_i <= NSA_WINDOW)
            for h in range(NSA_HEADS):
                s = _nt(qh[h], k) - ALIBI[h] * dist
                _online_step(s, valid, v, m_ref, l_ref, acc_ref, h)
            return carry

        lax.fori_loop(first_kb, qi + 1, body, 0)

    branch(0, 0, NSA_DK, True)
    for h in range(NSA_HEADS):
        o_ref[:, h * NSA_DK:(h + 1) * NSA_DK] += gate(1, h) * (acc_ref[h] / jnp.maximum(l_ref[h], 1e-30))
    branch(jnp.maximum(qi - NSA_WINDOW // K_BLOCK, 0), 2 * NSA_DK, 3 * NSA_DK, False)
    for h in range(NSA_HEADS):
        o_ref[:, h * NSA_DK:(h + 1) * NSA_DK] += gate(2, h) * (acc_ref[h] / jnp.maximum(l_ref[h], 1e-30))


def _nsa_prompt(z, kc, vc, kvb, *, batch, seq):
    nq = seq // Q_BLOCK
    nc = seq // NSA_L_CMP
    n_sb = -(-seq // NSA_L_SEL)
    expand = np.zeros((nc, seq), np.float32)
    for j in range(n_sb):
        expand[2 * j, j * NSA_L_SEL:(j + 1) * NSA_L_SEL] = 1.0
    return pl.pallas_call(
        functools.partial(_nsa_prompt_kernel, seq=seq, n_sel=min(NSA_TOPN, n_sb)),
        grid=(batch, nq),
        in_specs=[pl.BlockSpec((Q_BLOCK, 512), lambda b, i: (b * nq + i, Z_NSA_Q // 512)),
                  pl.BlockSpec((Q_BLOCK, LANES), lambda b, i: (b * nq + i, Z_MISC // LANES)),
                  pl.BlockSpec((nc, LANES), lambda b, i: (b, 0)),
                  pl.BlockSpec((nc, LANES), lambda b, i: (b, 0)),
                  pl.BlockSpec((seq, 512), lambda b, i: (b, 0)),
                  pl.BlockSpec((nc, seq), lambda b, i: (0, 0))],
        out_specs=pl.BlockSpec((Q_BLOCK, 512), lambda b, i: (b * nq + i, 0)),
        out_shape=jax.ShapeDtypeStruct((batch * seq, 512), F32),
        scratch_shapes=[pltpu.VMEM((Q_BLOCK, seq), F32)] + [pltpu.VMEM((NSA_HEADS, Q_BLOCK, LANES), F32)] * 3,
        compiler_params=_params(2),
        name="nsa_prompt",
    )(z, z, kc, vc, kvb, jnp.asarray(expand, BF16))


def _silu(x):
    return x * jax.nn.sigmoid(x)


def _mix_and_project(x, y_pool, olat, y_conv, y_nsa, g_pool, g_mla, g_conv, g_nsa, wuv_ref, wout_ref):
    y_mla = jnp.concatenate(
        [_nn(olat[:, h * MLA_KV_LORA:(h + 1) * MLA_KV_LORA].astype(BF16), wuv_ref[h]) for h in range(MLA_HEADS)],
        axis=1)
    mix = jnp.concatenate([y_pool * _silu(g_pool), y_mla * _silu(g_mla),
                           y_conv * _silu(g_conv), y_nsa * _silu(g_nsa)], axis=1)
    return x + _nn(mix.astype(BF16), wout_ref[...])


def _pool_project(d, pw_ref, ps_ref):
    return jnp.concatenate(
        [_nn(d[:, g * POOL_CH:(g + 1) * POOL_CH].astype(BF16), pw_ref[g]) for g in range(len(POOL_WINDOWS))],
        axis=1) * ps_ref[...]


def _out_prompt_kernel(x_ref, u_ref, uh_ref, gp_ref, gm_ref, ch_ref, cc_ref, cb_ref, gc_ref, gn_ref,
                       chh_ref, cch_ref, olat_ref, ynsa_ref, pw_ref, ps_ref, wuv_ref, cw_ref, cbias_ref,
                       wout_ref, fg_ref, *out_refs, tb, seq, final):
    if final:
        xo_ref, y_ref, zc_ref = out_refs
    else:
        xo_ref, zc_ref = out_refs
    i = pl.program_id(0)
    t0 = (i * tb) % seq
    first = t0 == 0
    u = u_ref[...]
    full = jnp.concatenate([jnp.where(first, 0.0, uh_ref[...]), u], axis=0)
    t = (t0 + lax.broadcasted_iota(jnp.int32, (tb, POOL_CH), 0) + 1).astype(F32)
    sums = full
    means = []
    step = 1
    for g, w in enumerate(POOL_WINDOWS):
        while step < w:
            sums = sums + pltpu.roll(sums, step, 0)
            step *= 2
        sg = sums[:, g * POOL_CH:(g + 1) * POOL_CH]
        means.append(sg[POOL_HIST + 1:, :] / jnp.minimum(float(w), t))
    d = jnp.concatenate(means, axis=1) - u
    y_pool = _pool_project(d, pw_ref, ps_ref)
    zc = cc_ref[...] * ch_ref[...]
    zfull = jnp.concatenate([jnp.where(first, 0.0, cch_ref[...] * chh_ref[...]), zc], axis=0)
    cw = cw_ref[...]
    conv = (cw[0:1] * pltpu.roll(zfull, 2, 0)[8:] + cw[1:2] * pltpu.roll(zfull, 1, 0)[8:]
            + cw[2:3] * zc + cbias_ref[...])
    y_conv = cb_ref[...] * conv
    zc_ref[...] = zc[tb - 8:, :]
    out = _mix_and_project(x_ref[...], y_pool, olat_ref[...], y_conv, ynsa_ref[...],
                           gp_ref[...], gm_ref[...], gc_ref[...], gn_ref[...], wuv_ref, wout_ref)
    xo_ref[...] = out
    if final:
        y_ref[...] = _rms(out, fg_ref[...])


def _out_prompt(x, z, olat, ynsa, lw, final_gain, *, seq, tb, final):
    m = x.shape[0]
    zblk = lambda c: pl.BlockSpec((tb, 512), lambda i: (i, c // 512))
    halo16 = pl.BlockSpec((16, 512), lambda i: (jnp.maximum(i * (tb // 16) - 1, 0), Z_POOL_U // 512))
    halo8 = lambda c: pl.BlockSpec((8, 512), lambda i: (jnp.maximum(i * (tb // 8) - 1, 0), c // 512))
    row = lambda w: pl.BlockSpec((tb, w), lambda i: (i, 0))
    cst2 = lambda a, b: pl.BlockSpec((a, b), lambda i: (0, 0))
    cst3 = lambda a, b, c: pl.BlockSpec((a, b, c), lambda i: (0, 0, 0))
    in_specs = [row(D_MODEL), zblk(Z_POOL_U), halo16, zblk(Z_POOL_G), zblk(Z_MLA_G),
                zblk(Z_CONV_H), zblk(Z_CONV_C), zblk(Z_CONV_B), zblk(Z_CONV_G), zblk(Z_NSA_G),
                halo8(Z_CONV_H), halo8(Z_CONV_C), row(512), row(512),
                cst3(len(POOL_WINDOWS), POOL_CH, POOL_CH), cst2(1, D_GROUP),
                cst3(MLA_HEADS, MLA_KV_LORA, 128), cst2(CONV_WIDTH, D_GROUP), cst2(1, D_GROUP),
                cst2(D_MODEL, D_MODEL), cst2(1, D_MODEL)]
    out_specs = [row(D_MODEL)] + ([row(D_MODEL)] if final else []) + [pl.BlockSpec((8, 512), lambda i: (i, 0))]
    out_shape = ([jax.ShapeDtypeStruct((m, D_MODEL), F32)] * (2 if final else 1)
                 + [jax.ShapeDtypeStruct((m // tb * 8, 512), F32)])
    return pl.pallas_call(
        functools.partial(_out_prompt_kernel, tb=tb, seq=seq, final=final),
        grid=(m // tb,),
        in_specs=in_specs, out_specs=out_specs, out_shape=out_shape,
        compiler_params=_params(1),
        name="out_prompt_final" if final else "out_prompt",
    )(x, z, z, z, z, z, z, z, z, z, z, z, olat, ynsa, lw["pool_w"], lw["pool_scale"], lw["w_uv"],
      lw["conv_w"], lw["conv_b"], lw["w_out"], final_gain)


def _out_sample_kernel(x_ref, z_ref, pst_ref, c0_ref, c1_ref, olat_ref, ynsa_ref, pw_ref, ps_ref, wuv_ref,
                       cw_ref, cbias_ref, wout_ref, fg_ref, *out_refs, n_past, final):
    if final:
        xo_ref, y_ref, zc_ref = out_refs
    else:
        xo_ref, zc_ref = out_refs
    zcol = lambda c: z_ref[:, c:c + 512]
    u = zcol(Z_POOL_U)
    sums = u
    means = []
    back = 1
    for g, w in enumerate(POOL_WINDOWS):
        while back < w:
            sums = sums + pst_ref[POOL_HIST - back]
            back += 1
        means.append(sums[:, g * POOL_CH:(g + 1) * POOL_CH] / float(min(w, n_past + 1)))
    d = jnp.concatenate(means, axis=1) - u
    y_pool = _pool_project(d, pw_ref, ps_ref)
    zc = zcol(Z_CONV_C) * zcol(Z_CONV_H)
    cw = cw_ref[...]
    conv = cw[0:1] * c0_ref[...] + cw[1:2] * c1_ref[...] + cw[2:3] * zc + cbias_ref[...]
    y_conv = zcol(Z_CONV_B) * conv
    zc_ref[...] = zc
    out = _mix_and_project(x_ref[...], y_pool, olat_ref[...], y_conv, ynsa_ref[...],
                           zcol(Z_POOL_G), zcol(Z_MLA_G), zcol(Z_CONV_G), zcol(Z_NSA_G), wuv_ref, wout_ref)
    xo_ref[...] = out
    if final:
        y_ref[...] = _rms(out, fg_ref[...])


def _out_sample(x, z, pool_state_t, conv0, conv1, olat, ynsa, lw, final_gain, *, n_past, final):
    m = x.shape[0]
    n_out = 3 if final else 2
    out_shape = ([jax.ShapeDtypeStruct((m, D_MODEL), F32)] * (n_out - 1) + [jax.ShapeDtypeStruct((m, 512), F32)])
    return pl.pallas_call(
        functools.partial(_out_sample_kernel, n_past=n_past, final=final),
        out_shape=out_shape,
        compiler_params=pltpu.CompilerParams(vmem_limit_bytes=VMEM_LIMIT),
        name="out_sample_final" if final else "out_sample",
    )(x, z, pool_state_t, conv0, conv1, olat, ynsa, lw["pool_w"], lw["pool_scale"], lw["w_uv"],
      lw["conv_w"], lw["conv_b"], lw["w_out"], final_gain)


def _heads_to_row(o_t):
    o = jnp.transpose(o_t)
    return jnp.concatenate([o[h:h + 1, :] for h in range(NSA_HEADS)], axis=1)


def _sample_paged_kernel(pt_ref, *refs, npp, n_chunks, n_past):
    del pt_ref
    lat_refs = refs[0:npp]
    kr_refs = refs[npp:2 * npp]
    ck_refs = refs[2 * npp:3 * npp]
    cv_refs = refs[3 * npp:4 * npp]
    (qm_ref, qr_ref, qn_ref, latn_ref, krn_ref, gk_ref, gv_ref,
     olat_ref, ocmp_ref, imp_ref, m_ref, l_ref, acc_ref, kc_ref, vc_ref) = refs[4 * npp:]
    c = pl.program_id(1)

    @pl.when(c == 0)
    def _():
        m_ref[...] = jnp.full(m_ref.shape, NEG_INF, F32)
        l_ref[...] = jnp.zeros(l_ref.shape, F32)
        acc_ref[...] = jnp.zeros(acc_ref.shape, F32)

    qm = qm_ref[...]
    qr = qr_ref[:, :MLA_ROPE]
    lats = [r[...].astype(BF16) for r in lat_refs]
    scores = [_nt(lats[i], qm) + _nt(kr_refs[i][...].astype(BF16), qr) for i in range(npp)]
    m_c = functools.reduce(jnp.maximum, [jnp.max(s, axis=0, keepdims=True) for s in scores])
    m_prev = m_ref[0:1, :]
    m_new = jnp.maximum(m_prev, m_c)
    alpha = jnp.exp(m_prev - m_new)
    acc = acc_ref[...] * alpha
    lsum = l_ref[0:1, :] * alpha
    for i in range(npp):
        p = jnp.exp(scores[i] - m_new)
        lsum = lsum + jnp.sum(p, axis=0, keepdims=True)
        acc = acc + _tn(lats[i], p.astype(BF16))
    m_ref[0:1, :] = m_new
    l_ref[0:1, :] = lsum
    acc_ref[...] = acc

    bpp = PAGE_SIZE // NSA_L_CMP
    gk = gk_ref[...][None]
    gv = gv_ref[...][None]
    kcs = [jnp.sum(r[...].reshape(bpp, NSA_L_CMP, NSA_DK) * gk, axis=1) for r in ck_refs]
    vcs = [jnp.sum(r[...].reshape(bpp, NSA_L_CMP, NSA_DK) * gv, axis=1) for r in cv_refs]
    r0 = pl.multiple_of(c * (npp * bpp), npp * bpp)
    kc_ref[pl.ds(r0, npp * bpp), :] = jnp.concatenate(kcs, axis=0) * (1.0 / NSA_L_CMP)
    vc_ref[pl.ds(r0, npp * bpp), :] = jnp.concatenate(vcs, axis=0) * (1.0 / NSA_L_CMP)

    @pl.when(c == n_chunks - 1)
    def _():
        latn = latn_ref[...].astype(BF16)
        is_new = lax.broadcasted_iota(jnp.int32, (16, LANES), 0) == 0
        s_n = jnp.where(is_new, _nt(latn, qm) + _nt(krn_ref[:, :MLA_ROPE].astype(BF16), qr), NEG_INF)
        m_f = jnp.maximum(m_new, jnp.max(s_n, axis=0, keepdims=True))
        a_f = jnp.exp(m_new - m_f)
        p_n = jnp.where(is_new, jnp.exp(s_n - m_f), 0.0)
        l_f = lsum * a_f + jnp.sum(p_n, axis=0, keepdims=True)
        acc_f = acc * a_f + _tn(latn, p_n.astype(BF16))
        olat_ref[...] = _heads_to_row(acc_f / jnp.maximum(l_f, 1e-30))

        nblk = n_past // NSA_L_CMP
        s = _nt(qn_ref[...], kc_ref[...].astype(BF16))[0:16, :]
        hrow = lax.broadcasted_iota(jnp.int32, (16, nblk), 0)
        bend = lax.broadcasted_iota(jnp.int32, (16, nblk), 1) * NSA_L_CMP + (NSA_L_CMP - 1)
        slope = functools.reduce(lambda a, h: jnp.where(hrow == h, ALIBI[h], a), range(NSA_HEADS),
                                 jnp.zeros((16, nblk), F32))
        valid = bend <= n_past
        s = jnp.where(valid, s - slope * (n_past - bend).astype(F32), NEG_INF)
        e = jnp.where(valid, jnp.exp(s - jnp.max(s, axis=1, keepdims=True)), 0.0)
        p = e / jnp.maximum(jnp.sum(e, axis=1, keepdims=True), 1e-30)
        imp_ref[...] = jnp.sum(jnp.where(hrow < NSA_HEADS, p, 0.0), axis=0, keepdims=True)
        o_c = _nn(p.astype(BF16), vc_ref[...].astype(BF16))
        ocmp_ref[...] = jnp.concatenate([o_c[h:h + 1, :] for h in range(NSA_HEADS)], axis=1)


def _sample_paged(page_table, c_lat, c_kr, c_ck, c_cv, qm, qr, qn, latn, krn, gk, gv, *, layer, npp):
    batch, n_pages = page_table.shape
    n_chunks = n_pages // npp
    n_past = n_pages * PAGE_SIZE
    nblk = n_past // NSA_L_CMP

    def page_spec(width, i):
        return pl.BlockSpec((None, None, PAGE_SIZE, width),
                            lambda b, c, pt: (layer, pt[b, c * npp + i], 0, 0))

    per_b3 = lambda r, w: pl.BlockSpec((None, r, w), lambda b, c, pt: (b, 0, 0))
    cst2 = lambda a, w: pl.BlockSpec((a, w), lambda b, c, pt: (0, 0))
    in_specs = ([page_spec(MLA_KV_LORA, i) for i in range(npp)] + [page_spec(MLA_ROPE, i) for i in range(npp)]
                + [page_spec(NSA_DK, i) for i in range(npp)] + [page_spec(NSA_DK, i) for i in range(npp)]
                + [per_b3(LANES, LANES), per_b3(LANES, LANES), per_b3(LANES, LANES),
                   per_b3(16, LANES), per_b3(16, LANES), cst2(NSA_L_CMP, NSA_DK), cst2(NSA_L_CMP, NSA_DK)])
    out_specs = [per_b3(1, 512), per_b3(1, 512), per_b3(1, nblk)]
    out_shape = [jax.ShapeDtypeStruct((batch, 1, 512), F32), jax.ShapeDtypeStruct((batch, 1, 512), F32),
                 jax.ShapeDtypeStruct((batch, 1, nblk), F32)]
    grid_spec = pltpu.PrefetchScalarGridSpec(
        num_scalar_prefetch=1, grid=(batch, n_chunks), in_specs=in_specs, out_specs=out_specs,
        scratch_shapes=[pltpu.VMEM((8, LANES), F32), pltpu.VMEM((8, LANES), F32),
                        pltpu.VMEM((MLA_KV_LORA, LANES), F32),
                        pltpu.VMEM((nblk, NSA_DK), F32), pltpu.VMEM((nblk, NSA_DK), F32)])
    return pl.pallas_call(
        functools.partial(_sample_paged_kernel, npp=npp, n_chunks=n_chunks, n_past=n_past),
        grid_spec=grid_spec, out_shape=out_shape,
        compiler_params=_params(2),
        name="sample_paged",
    )(page_table, *([c_lat] * npp + [c_kr] * npp + [c_ck] * npp + [c_cv] * npp),
      qm, qr, qn, latn, krn, gk, gv)


def _select_kernel(imp_ref, idx_ref, *, n_past, n_sel):
    rows, nblk = imp_ref.shape
    ratio = NSA_L_SEL // NSA_L_CMP
    n_sb = -(-(n_past + 1) // NSA_L_SEL)
    n_in = nblk // ratio
    cur = n_past // NSA_L_SEL
    x = imp_ref[...]
    pair = x + pltpu.roll(x, nblk - 1, 1)
    lane = lax.broadcasted_iota(jnp.int32, (rows, nblk + LANES), 1)
    blk = jnp.where(lane < nblk, lane >> 1, n_in + lane - nblk)
    exists = ((lane < nblk) & ((lane & 1) == 0)) | ((lane >= nblk) & (blk < n_sb))
    imp = jnp.concatenate([pair, jnp.zeros((rows, LANES), F32)], axis=1)
    forced = (blk == 0) | (blk == cur) | (blk == cur - 1)
    score = jnp.where(blk * NSA_L_SEL <= n_past, jnp.where(forced, FORCED_SCORE, imp), -FORCED_SCORE)
    score = jnp.where(exists, score, -3e38)
    lane_f = lane.astype(F32)
    out_lane = lax.broadcasted_iota(jnp.int32, (rows, LANES), 1)
    out = jnp.zeros((rows, LANES), jnp.int32)
    for r in range(n_sel):
        top = jnp.max(score, axis=1, keepdims=True)
        first = jnp.min(jnp.where(score == top, lane_f, 1e9), axis=1, keepdims=True)
        first_i = first.astype(jnp.int32)
        chosen = jnp.where(first_i < nblk, first_i >> 1, n_in + first_i - nblk)
        out = jnp.where(out_lane == r, chosen, out)
        score = jnp.where(lane_f == first, -3e38, score)
    idx_ref[...] = out


def _select(imp, *, n_past, n_sel):
    rows = imp.shape[0]
    return pl.pallas_call(
        functools.partial(_select_kernel, n_past=n_past, n_sel=n_sel),
        out_shape=jax.ShapeDtypeStruct((rows, LANES), jnp.int32),
        compiler_params=pltpu.CompilerParams(vmem_limit_bytes=VMEM_LIMIT),
        name="select",
    )(imp)


def _sample_nsa_kernel(ph_ref, hf_ref, blk_ref, *refs, n_sel, n_past, wb):
    del ph_ref, hf_ref
    k_refs = refs[0:n_sel]
    v_refs = refs[n_sel:2 * n_sel]
    (qn_ref, ksn_ref, vsn_ref, wk_ref, wv_ref, kwn_ref, vwn_ref, misc_ref, ocmp_ref, o_ref) = refs[2 * n_sel:]
    b = pl.program_id(0)
    qn = qn_ref[...]
    lane = lax.broadcasted_iota(jnp.int32, (1, LANES), 1)
    slope = functools.reduce(lambda a, h: jnp.where(lane == h, ALIBI[h], a), range(NSA_HEADS),
                             jnp.zeros((1, LANES), F32))
    is_new = lax.broadcasted_iota(jnp.int32, (16, LANES), 0) == 0

    def attend(keys, values, dists, valids, k_new, v_new):
        scores = []
        for k, d, ok in zip(keys, dists, valids):
            scores.append(jnp.where(ok, _nt(k.astype(BF16), qn) - slope * d, NEG_INF))
        s_n = jnp.where(is_new, _nt(k_new.astype(BF16), qn), NEG_INF)
        m = functools.reduce(jnp.maximum, [jnp.max(s, axis=0, keepdims=True) for s in scores + [s_n]])
        p_n = jnp.where(is_new, jnp.exp(s_n - m), 0.0)
        lsum = jnp.sum(p_n, axis=0, keepdims=True)
        acc = _tn(v_new.astype(BF16), p_n.astype(BF16))
        for s, v, ok in zip(scores, values, valids):
            p = jnp.where(ok, jnp.exp(s - m), 0.0)
            lsum = lsum + jnp.sum(p, axis=0, keepdims=True)
            acc = acc + _tn(v.astype(BF16), p.astype(BF16))
        return _heads_to_row(acc / jnp.maximum(lsum, 1e-30))

    rows = lax.broadcasted_iota(jnp.int32, (NSA_L_SEL, LANES), 0)
    dists, valids = [], []
    for i in range(n_sel):
        blk = blk_ref[b, i]
        base = jnp.where(blk < n_past // NSA_L_SEL, n_past - blk * NSA_L_SEL, -1)
        d = base - rows
        dists.append(d.astype(F32))
        valids.append(d >= 0)
    o_sel = attend([r[...] for r in k_refs], [r[...] for r in v_refs], dists, valids, ksn_ref[...], vsn_ref[...])
    wd = wb - lax.broadcasted_iota(jnp.int32, (wb, LANES), 0)
    o_win = attend([wk_ref[...]], [wv_ref[...]], [wd.astype(F32)], [(wd >= 0) & (wd <= NSA_WINDOW)],
                   kwn_ref[...], vwn_ref[...])
    gates = jax.nn.sigmoid(misc_ref[...])
    o_cmp = ocmp_ref[...]
    pieces = []
    for h in range(NSA_HEADS):
        sl = slice(h * NSA_DK, (h + 1) * NSA_DK)
        g = [gates[:, MISC_GATE0 + NSA_HEADS * j + h:MISC_GATE0 + NSA_HEADS * j + h + 1] for j in range(3)]
        pieces.append(g[0] * o_cmp[:, sl] + g[1] * o_sel[:, sl] + g[2] * o_win[:, sl])
    o_ref[...] = jnp.concatenate(pieces, axis=1)


def _sample_nsa(phys, half, blk, c_sk, c_sv, qn, ksn, vsn, win_k, win_v, kwn, vwn, misc, ocmp, *,
                layer, n_past):
    batch, n_sel = blk.shape
    wb = win_k.shape[2]

    def half_page(i):
        return pl.BlockSpec((None, None, NSA_L_SEL, NSA_DK),
                            lambda b, ph, hf, bl: (layer, ph[b, i], hf[b, i], 0))

    per_b3 = lambda r, w: pl.BlockSpec((None, r, w), lambda b, ph, hf, bl: (b, 0, 0))
    win = pl.BlockSpec((None, None, wb, NSA_DK), lambda b, ph, hf, bl: (layer, b, 0, 0))
    in_specs = ([half_page(i) for i in range(n_sel)] + [half_page(i) for i in range(n_sel)]
                + [per_b3(LANES, LANES), per_b3(16, LANES), per_b3(16, LANES), win, win,
                   per_b3(16, LANES), per_b3(16, LANES), per_b3(1, LANES), per_b3(1, 512)])
    grid_spec = pltpu.PrefetchScalarGridSpec(
        num_scalar_prefetch=3, grid=(batch,), in_specs=in_specs, out_specs=per_b3(1, 512))
    return pl.pallas_call(
        functools.partial(_sample_nsa_kernel, n_sel=n_sel, n_past=n_past, wb=wb),
        grid_spec=grid_spec, out_shape=jax.ShapeDtypeStruct((batch, 1, 512), F32),
        compiler_params=_params(1),
        name="sample_nsa",
    )(phys, half, blk, *([c_sk] * n_sel + [c_sv] * n_sel), qn, ksn, vsn, win_k, win_v, kwn, vwn, misc, ocmp)


def _rope_tables(pos):
    half = MLA_ROPE // 2
    inv = ROPE_THETA ** (-jnp.arange(half, dtype=F32) / half)
    ang = pos.astype(F32)[:, None] * inv
    cos, sin = jnp.cos(ang), jnp.sin(ang)
    reps = LANES // MLA_ROPE
    return (jnp.tile(jnp.concatenate([cos, cos], axis=1), (1, reps)),
            jnp.tile(jnp.concatenate([-sin, sin], axis=1), (1, reps)))


def _layer_weights(l, norm_gain, w_in, w_out, pool_w, pool_scale, mla_q_norm, mla_kv_norm, mla_w_uq, mla_w_uk,
                   mla_w_uv, conv_w, conv_b, nsa_cmp_pos_k, nsa_cmp_pos_v):
    w = w_in[l]
    o_kr, o_mg, o_q, o_kc, o_bg, o_ng, o_end = 1536, 1568, 4128, 4640, 5408, 5420, 5932
    w_p = jnp.concatenate(
        [w[:, :o_kr], w[:, o_mg:o_q + 512], w[:, o_ng:o_end], w[:, o_kc:o_bg], w[:, o_kr:o_mg],
         w[:, o_bg:o_ng], jnp.zeros((D_MODEL, Z_WIDTH - Z_MISC - MLA_ROPE - 3 * NSA_HEADS), w.dtype)],
        axis=1).astype(BF16)
    uq = mla_w_uq[l].reshape(MLA_Q_LORA, MLA_HEADS, MLA_NOPE + MLA_ROPE)
    return {
        "norm": norm_gain[l][None], "w_in": w_p, "w_out": w_out[l].astype(BF16),
        "pool_w": pool_w[l].astype(BF16), "pool_scale": pool_scale[l][None],
        "q_norm": mla_q_norm[l][None], "kv_norm": mla_kv_norm[l][None],
        "w_uq_nope": uq[:, :, :MLA_NOPE].reshape(MLA_Q_LORA, -1).astype(BF16),
        "w_uq_rope": uq[:, :, MLA_NOPE:].reshape(MLA_Q_LORA, -1).astype(BF16),
        "w_ukT": jnp.transpose(mla_w_uk[l], (1, 2, 0)).astype(BF16),
        "w_uv": jnp.transpose(mla_w_uv[l], (1, 0, 2)).astype(BF16),
        "conv_w": conv_w[l], "conv_b": conv_b[l][None],
        "gain_k": nsa_cmp_pos_k[l], "gain_v": nsa_cmp_pos_v[l],
    }


def _pad_rows(a, rows):
    return jnp.pad(a, ((0, 0), (0, rows - a.shape[1]), (0, 0)))


def kernel(x_prompt, x_sample, cache_mla_latent, cache_mla_krope, cache_nsa_cmp_k, cache_nsa_cmp_v,
           cache_nsa_sel_k, cache_nsa_sel_v, state_nsa_win_k, state_nsa_win_v, state_conv, state_pool,
           page_table, norm_gain, w_in, w_out, pool_w, pool_scale, mla_q_norm, mla_kv_norm,
           mla_w_uq, mla_w_uk, mla_w_uv, conv_w, conv_b, nsa_cmp_pos_k, nsa_cmp_pos_v, final_norm):
    batch, seq, _ = x_prompt.shape
    dec, dec_seq, _ = x_sample.shape
    depth = norm_gain.shape[0]
    n_pages = page_table.shape[1]
    n_past = n_pages * PAGE_SIZE
    wb = state_nsa_win_k.shape[2]
    assert dec_seq == 1 and seq % 256 == 0 and dec % 8 == 0
    n_sel = min(NSA_TOPN, -(-(n_past + dec_seq) // NSA_L_SEL))
    npp = 8 if n_pages % 8 == 0 else n_pages

    cos_p, sin_p = _rope_tables(jnp.arange(seq, dtype=jnp.int32))
    cos_s, sin_s = _rope_tables(jnp.full((dec,), n_past, jnp.int32))
    final_gain = final_norm[None]

    xp = x_prompt.reshape(batch * seq, D_MODEL)
    xs = x_sample.reshape(dec, D_MODEL)
    new_p, new_s = [], []
    yp = ys = None
    for l in range(depth):
        final = l == depth - 1
        lw = _layer_weights(l, norm_gain, w_in, w_out, pool_w, pool_scale, mla_q_norm, mla_kv_norm, mla_w_uq,
                            mla_w_uk, mla_w_uv, conv_w, conv_b, nsa_cmp_pos_k, nsa_cmp_pos_v)
        zp = _inproj(xp, lw["norm"], lw["w_in"], tm=512, tn=1536)
        qcat, kcat, lat, kr, kvb, kc, vc = _prep(zp, cos_p, sin_p, lw, tb=256, rows_per_table=seq, with_nsa=True)
        olat = _mla_prompt(qcat, kcat, batch=batch, seq=seq)
        ynsa = _nsa_prompt(zp, kc, vc, kvb, batch=batch, seq=seq)
        outs = _out_prompt(xp, zp, olat, ynsa, lw, final_gain, seq=seq, tb=256, final=final)
        if final:
            xp, yp, zc_tail = outs
        else:
            xp, zc_tail = outs
        z3 = zp.reshape(batch, seq, Z_WIDTH)
        wbp = min(NSA_WINDOW, seq)
        zc_last = zc_tail.reshape(batch, seq // 256, 8, 512)[:, -1]
        new_p.append((lat.reshape(batch, seq, LANES), kr.reshape(batch, seq, LANES)[:, :, :MLA_ROPE],
                      z3[:, :, Z_KC:Z_KC + NSA_DK], z3[:, :, Z_VC:Z_VC + NSA_DK],
                      z3[:, :, Z_KS:Z_KS + NSA_DK], z3[:, :, Z_VS:Z_VS + NSA_DK],
                      z3[:, seq - wbp:, Z_KW:Z_KW + NSA_DK], z3[:, seq - wbp:, Z_VW:Z_VW + NSA_DK],
                      zc_last[:, 8 - (CONV_WIDTH - 1):], z3[:, seq - POOL_HIST:, Z_POOL_U:Z_POOL_U + D_GROUP]))
        zs = _inproj(xs, lw["norm"], lw["w_in"], tm=dec, tn=1536)
        qcat_s, _, lat_s, kr_s = _prep(zs, cos_s, sin_s, lw, tb=dec, rows_per_table=dec, with_nsa=False)
        q4 = qcat_s.reshape(dec, MLA_HEADS, 256)
        qm = _pad_rows(q4[:, :, :LANES], LANES)
        qr = _pad_rows(q4[:, :, LANES:], LANES)
        qn = _pad_rows((zs[:, Z_NSA_Q:Z_NSA_Q + D_GROUP] * NSA_SCALE).astype(BF16).reshape(dec, NSA_HEADS, NSA_DK),
                       LANES)
        new_row = lambda a: _pad_rows(a[:, None, :], 16)
        olat_s, ocmp_s, imp = _sample_paged(
            page_table, cache_mla_latent, cache_mla_krope, cache_nsa_cmp_k, cache_nsa_cmp_v,
            qm, qr, qn, new_row(lat_s), new_row(kr_s), lw["gain_k"], lw["gain_v"], layer=l, npp=npp)
        blk = _select(imp.reshape(dec, -1), n_past=n_past, n_sel=n_sel)[:, :n_sel]
        page = jnp.minimum(blk // (PAGE_SIZE // NSA_L_SEL), n_pages - 1)
        phys = jnp.take_along_axis(page_table, page, axis=1)
        half = blk % (PAGE_SIZE // NSA_L_SEL)
        zcol = lambda c, w=NSA_DK: zs[:, c:c + w]
        ynsa_s = _sample_nsa(
            phys, half, blk, cache_nsa_sel_k, cache_nsa_sel_v, qn, new_row(zcol(Z_KS)), new_row(zcol(Z_VS)),
            state_nsa_win_k, state_nsa_win_v, new_row(zcol(Z_KW)), new_row(zcol(Z_VW)),
            zcol(Z_MISC, LANES)[:, None, :], ocmp_s, layer=l, n_past=n_past)
        outs = _out_sample(xs, zs, jnp.swapaxes(state_pool[l], 0, 1), state_conv[l, :, 0], state_conv[l, :, 1],
                           olat_s.reshape(dec, 512), ynsa_s.reshape(dec, 512), lw, final_gain,
                           n_past=n_past, final=final)
        if final:
            xs, ys, zc_s = outs
        else:
            xs, zc_s = outs
        app = lambda old, new: jnp.concatenate([old, new[:, None, :]], axis=1)[:, -old.shape[1]:]
        new_s.append((lat_s[:, None, :], kr_s[:, None, :MLA_ROPE],
                      zcol(Z_KC)[:, None], zcol(Z_VC)[:, None], zcol(Z_KS)[:, None], zcol(Z_VS)[:, None],
                      app(state_nsa_win_k[l], zcol(Z_KW)), app(state_nsa_win_v[l], zcol(Z_VW)),
                      app(state_conv[l], zc_s), app(state_pool[l], zcol(Z_POOL_U, D_GROUP))))
    stack = lambda rows: tuple(jnp.stack([r[i] for r in rows]) for i in range(len(rows[0])))
    return ((yp.reshape(batch, seq, D_MODEL), ys.reshape(dec, 1, D_MODEL)) + stack(new_p) + stack(new_s))
```

```python
import functools

import numpy as np
import jax
import jax.numpy as jnp
from jax import lax
from jax.experimental import pallas as pl
from jax.experimental.pallas import tpu as pltpu

F32 = jnp.float32
BF16 = jnp.bfloat16

D_MODEL = 2048
D_GROUP = 512
POOL_WINDOWS = (2, 4, 8, 16)
POOL_CH = 128
POOL_HIST = 15
MLA_HEADS = 4
MLA_NOPE = 128
MLA_ROPE = 32
MLA_Q_LORA = 384
MLA_KV_LORA = 128
ROPE_THETA = 10000.0
MLA_SCALE = (MLA_NOPE + MLA_ROPE) ** -0.5
CONV_WIDTH = 3
NSA_HEADS = 4
NSA_DK = 128
NSA_L_CMP = 32
NSA_L_SEL = 64
NSA_TOPN = 16
NSA_WINDOW = 512
NSA_SCALE = NSA_DK ** -0.5
FORCED_SCORE = 1e6
RMS_EPS = 1e-6
NEG_INF = -1e30
M_INIT = -5e29
PAGE_SIZE = 128
ALIBI = tuple(2.0 ** (-8.0 * (h + 1) / NSA_HEADS) for h in range(NSA_HEADS))

LANES = 128
Q_BLOCK = 128
K_BLOCK = 256
VMEM_LIMIT = 56 * 1024 * 1024

Z_POOL_U, Z_POOL_G = 0, 512
Z_CQ, Z_CKV = 1024, 1408
Z_MLA_G = 1536
Z_CONV_H, Z_CONV_C, Z_CONV_B, Z_CONV_G = 2048, 2560, 3072, 3584
Z_NSA_Q, Z_NSA_G = 4096, 4608
Z_KC, Z_VC, Z_KS, Z_VS, Z_KW, Z_VW = 5120, 5248, 5376, 5504, 5632, 5760
Z_MISC = 5888
Z_WIDTH = 6144
MISC_GATE0 = 32


def _nn(a, b):
    return jnp.dot(a, b, preferred_element_type=F32)


def _nt(a, b):
    return lax.dot_general(a, b, (((1,), (1,)), ((), ())), preferred_element_type=F32)


def _tn(a, b):
    return lax.dot_general(a, b, (((0,), (0,)), ((), ())), preferred_element_type=F32)


def _rms(x, g):
    return x * lax.rsqrt(jnp.mean(x * x, axis=-1, keepdims=True) + RMS_EPS) * g


def _params(n_axes):
    return pltpu.CompilerParams(dimension_semantics=("arbitrary",) * n_axes,
                                vmem_limit_bytes=VMEM_LIMIT)


def _inproj_kernel(x_ref, g_ref, w_ref, o_ref, xn_ref, *, tm):
    @pl.when(pl.program_id(1) == 0)
    def _():
        rows = 64 if tm % 64 == 0 else tm

        def body(i, c):
            r = pl.multiple_of(i * rows, rows)
            xn_ref[pl.ds(r, rows), :] = _rms(x_ref[pl.ds(r, rows), :], g_ref[...]).astype(BF16)
            return c
        lax.fori_loop(0, tm // rows, body, 0)

    o_ref[...] = _nn(xn_ref[...], w_ref[...])


def _inproj(x, g, w, *, tm, tn):
    m = x.shape[0]
    return pl.pallas_call(
        functools.partial(_inproj_kernel, tm=tm),
        grid=(m // tm, Z_WIDTH // tn),
        in_specs=[pl.BlockSpec((tm, D_MODEL), lambda i, j: (i, 0)),
                  pl.BlockSpec((1, D_MODEL), lambda i, j: (0, 0)),
                  pl.BlockSpec((D_MODEL, tn), lambda i, j: (0, j))],
        out_specs=pl.BlockSpec((tm, tn), lambda i, j: (i, j)),
        out_shape=jax.ShapeDtypeStruct((m, Z_WIDTH), F32),
        scratch_shapes=[pltpu.VMEM((tm, D_MODEL), BF16)],
        compiler_params=_params(2),
        name="inproj",
    )(x, g, w)


def _rope128(x, cos, sin, lane):
    sw = jnp.where((lane & (MLA_ROPE - 1)) < MLA_ROPE // 2, pltpu.roll(x, LANES - MLA_ROPE // 2, 1),
                   pltpu.roll(x, MLA_ROPE // 2, 1))
    return x * cos + sw * sin


def _prep_kernel(zq_ref, zn_ref, cos_ref, sin_ref, qg_ref, kg_ref, wqn_ref, wqr_ref, wuk_ref,
                 gk_ref, gv_ref, *out_refs, tb, with_nsa):
    if with_nsa:
        qcat_ref, kcat_ref, lat_ref, kr_ref, kvb_ref, kc_ref, vc_ref = out_refs
    else:
        qcat_ref, kcat_ref, lat_ref, kr_ref = out_refs
    zq = zq_ref[...]
    cos = cos_ref[...]
    sin = sin_ref[...]
    lane = lax.broadcasted_iota(jnp.int32, (tb, LANES), 1)
    cqn = _rms(zq[:, :MLA_Q_LORA], qg_ref[...]).astype(BF16)
    lat = _rms(zq[:, MLA_Q_LORA:], kg_ref[...])
    q_nope = _nn(cqn, wqn_ref[...])
    q_rope = _rope128(_nn(cqn, wqr_ref[...]), cos, sin, lane) * MLA_SCALE
    misc = zn_ref[:, Z_MISC - Z_KC:Z_MISC - Z_KC + LANES]
    kr = jnp.where(lane < MLA_ROPE, _rope128(misc, cos, sin, lane), 0.0)
    for h in range(MLA_HEADS):
        q_lat = _nn(q_nope[:, h * MLA_NOPE:(h + 1) * MLA_NOPE].astype(BF16), wuk_ref[h]) * MLA_SCALE
        qcat_ref[:, h * 256:h * 256 + LANES] = q_lat.astype(BF16)
        qr = q_rope if h == 0 else pltpu.roll(q_rope, LANES - MLA_ROPE * h, 1)
        qcat_ref[:, h * 256 + LANES:(h + 1) * 256] = jnp.where(lane < MLA_ROPE, qr, 0.0).astype(BF16)
    kcat_ref[:, :LANES] = lat.astype(BF16)
    kcat_ref[:, LANES:] = kr.astype(BF16)
    lat_ref[...] = lat
    kr_ref[...] = kr
    if with_nsa:
        kvb_ref[...] = zn_ref[:, Z_KS - Z_KC:Z_VW - Z_KC + LANES].astype(BF16)
        nb = tb // NSA_L_CMP
        kcb = zn_ref[:, 0:LANES].reshape(nb, NSA_L_CMP, LANES) * gk_ref[...][None]
        vcb = zn_ref[:, LANES:2 * LANES].reshape(nb, NSA_L_CMP, LANES) * gv_ref[...][None]
        kc_ref[...] = jnp.sum(kcb, axis=1) * (1.0 / NSA_L_CMP)
        vc_ref[...] = jnp.sum(vcb, axis=1) * (1.0 / NSA_L_CMP)


def _prep(z, cos, sin, lw, *, tb, rows_per_table, with_nsa):
    m = z.shape[0]
    nt = rows_per_table // tb
    row = lambda i: (i, 0)
    full2 = lambda i: (0, 0)
    in_specs = [pl.BlockSpec((tb, 512), lambda i: (i, Z_CQ // 512)),
                pl.BlockSpec((tb, 1024), lambda i: (i, Z_KC // 1024)),
                pl.BlockSpec((tb, LANES), lambda i: (i % nt, 0)),
                pl.BlockSpec((tb, LANES), lambda i: (i % nt, 0)),
                pl.BlockSpec((1, MLA_Q_LORA), full2),
                pl.BlockSpec((1, MLA_KV_LORA), full2),
                pl.BlockSpec((MLA_Q_LORA, MLA_HEADS * MLA_NOPE), full2),
                pl.BlockSpec((MLA_Q_LORA, MLA_HEADS * MLA_ROPE), full2),
                pl.BlockSpec((MLA_HEADS, MLA_NOPE, MLA_KV_LORA), lambda i: (0, 0, 0)),
                pl.BlockSpec((NSA_L_CMP, NSA_DK), full2),
                pl.BlockSpec((NSA_L_CMP, NSA_DK), full2)]
    out_specs = [pl.BlockSpec((tb, 1024), row), pl.BlockSpec((tb, 256), row),
                 pl.BlockSpec((tb, LANES), row), pl.BlockSpec((tb, LANES), row)]
    out_shape = [jax.ShapeDtypeStruct((m, 1024), BF16), jax.ShapeDtypeStruct((m, 256), BF16),
                 jax.ShapeDtypeStruct((m, LANES), F32), jax.ShapeDtypeStruct((m, LANES), F32)]
    if with_nsa:
        nb = tb // NSA_L_CMP
        out_specs += [pl.BlockSpec((tb, 512), row), pl.BlockSpec((nb, LANES), row), pl.BlockSpec((nb, LANES), row)]
        out_shape += [jax.ShapeDtypeStruct((m, 512), BF16),
                      jax.ShapeDtypeStruct((m // NSA_L_CMP, LANES), F32),
                      jax.ShapeDtypeStruct((m // NSA_L_CMP, LANES), F32)]
    return pl.pallas_call(
        functools.partial(_prep_kernel, tb=tb, with_nsa=with_nsa),
        grid=(m // tb,),
        in_specs=in_specs, out_specs=out_specs, out_shape=out_shape,
        compiler_params=_params(1),
        name="prep_nsa" if with_nsa else "prep",
    )(z, z, cos, sin, lw["q_norm"], lw["kv_norm"], lw["w_uq_nope"], lw["w_uq_rope"], lw["w_ukT"],
      lw["gain_k"], lw["gain_v"])


def _online_step(s, v, m_ref, l_ref, acc_ref):
    s0, s1 = s[:, :LANES], s[:, LANES:]
    m_prev = m_ref[...]
    m_new = jnp.maximum(m_prev, jnp.max(jnp.maximum(s0, s1), axis=1, keepdims=True))
    alpha = jnp.exp(m_prev - m_new)
    p0 = jnp.exp(s0 - m_new)
    p1 = jnp.exp(s1 - m_new)
    l_ref[...] = alpha * l_ref[...] + jnp.sum(p0 + p1, axis=1, keepdims=True)
    p = jnp.concatenate([p0, p1], axis=1).astype(BF16)
    acc_ref[...] = alpha * acc_ref[...] + _nn(p, v)
    m_ref[...] = m_new


def _reset_state(m_ref, l_ref, acc_ref):
    m_ref[...] = jnp.full(m_ref.shape, M_INIT, F32)
    l_ref[...] = jnp.zeros(l_ref.shape, F32)
    acc_ref[...] = jnp.zeros(acc_ref.shape, F32)


def _stack_heads(x):
    return jnp.concatenate([x] * NSA_HEADS, axis=0)


def _mla_prompt_kernel(q_ref, k_ref, o_ref, m_ref, l_ref, acc_ref):
    qi = pl.program_id(1)
    _reset_state(m_ref, l_ref, acc_ref)
    q_all = jnp.concatenate([q_ref[:, h * 256:(h + 1) * 256] for h in range(MLA_HEADS)], axis=0)
    rc = (lax.broadcasted_iota(jnp.int32, (Q_BLOCK, K_BLOCK), 0)
          - lax.broadcasted_iota(jnp.int32, (Q_BLOCK, K_BLOCK), 1))

    def body(kb, carry):
        k0 = pl.multiple_of(kb * K_BLOCK, K_BLOCK)
        k = k_ref[pl.ds(k0, K_BLOCK), :]
        pen = jnp.where(rc + (qi * Q_BLOCK - kb * K_BLOCK) >= 0, 0.0, NEG_INF)
        _online_step(_nt(q_all, k) + _stack_heads(pen), k[:, :MLA_KV_LORA], m_ref, l_ref, acc_ref)
        return carry

    lax.fori_loop(0, (qi * Q_BLOCK) // K_BLOCK + 1, body, 0)
    out = acc_ref[...] / jnp.maximum(l_ref[...], 1e-30)
    for h in range(MLA_HEADS):
        o_ref[:, h * LANES:(h + 1) * LANES] = out[h * Q_BLOCK:(h + 1) * Q_BLOCK]


def _mla_prompt(qcat, kcat, *, batch, seq):
    nq = seq // Q_BLOCK
    return pl.pallas_call(
        _mla_prompt_kernel,
        grid=(batch, nq),
        in_specs=[pl.BlockSpec((Q_BLOCK, 1024), lambda b, i: (b * nq + i, 0)),
                  pl.BlockSpec((seq, 256), lambda b, i: (b, 0))],
        out_specs=pl.BlockSpec((Q_BLOCK, 512), lambda b, i: (b * nq + i, 0)),
        out_shape=jax.ShapeDtypeStruct((batch * seq, 512), F32),
        scratch_shapes=[pltpu.VMEM((MLA_HEADS * Q_BLOCK, LANES), F32)] * 3,
        compiler_params=_params(2),
        name="mla_prompt",
    )(qcat, kcat)


def _nsa_prompt_kernel(q_ref, misc_ref, kc_ref, vc_ref, kvb_ref, e_ref, o_ref,
                       selmask_ref, m_ref, l_ref, acc_ref, *, seq, n_sel):
    qi = pl.program_id(1)
    t0 = qi * Q_BLOCK
    nc = seq // NSA_L_CMP
    q = q_ref[...] * NSA_SCALE
    qh = [q[:, h * NSA_DK:(h + 1) * NSA_DK].astype(BF16) for h in range(NSA_HEADS)]
    gates = jax.nn.sigmoid(misc_ref[...])

    def gate(j, h):
        c = MISC_GATE0 + NSA_HEADS * j + h
        return gates[:, c:c + 1]

    kc = kc_ref[...].astype(BF16)
    vc = vc_ref[...].astype(BF16)
    t_a = t0 + lax.broadcasted_iota(jnp.int32, (Q_BLOCK, nc), 0)
    bend_a = lax.broadcasted_iota(jnp.int32, (Q_BLOCK, nc), 1) * NSA_L_CMP + (NSA_L_CMP - 1)
    mask_a = bend_a <= t_a
    dist_a = (t_a - bend_a).astype(F32)
    t_b = t0 + lax.broadcasted_iota(jnp.int32, (nc, Q_BLOCK), 1)
    bend_b = lax.broadcasted_iota(jnp.int32, (nc, Q_BLOCK), 0) * NSA_L_CMP + (NSA_L_CMP - 1)
    mask_b = bend_b <= t_b
    dist_b = (t_b - bend_b).astype(F32)
    imp = jnp.zeros((nc, Q_BLOCK), F32)
    for h in range(NSA_HEADS):
        s_a = jnp.where(mask_a, _nt(qh[h], kc) - ALIBI[h] * dist_a, NEG_INF)
        e_a = jnp.where(mask_a, jnp.exp(s_a - jnp.max(s_a, axis=1, keepdims=True)), 0.0)
        p_a = e_a / jnp.maximum(jnp.sum(e_a, axis=1, keepdims=True), 1e-30)
        o_ref[:, h * NSA_DK:(h + 1) * NSA_DK] = gate(0, h) * _nn(p_a.astype(BF16), vc)
        s_b = jnp.where(mask_b, _nt(kc, qh[h]) - ALIBI[h] * dist_b, NEG_INF)
        e_b = jnp.where(mask_b, jnp.exp(s_b - jnp.max(s_b, axis=0, keepdims=True)), 0.0)
        imp = imp + e_b / jnp.maximum(jnp.sum(e_b, axis=0, keepdims=True), 1e-30)

    imp2 = imp + pltpu.roll(imp, nc - 1, 0)
    ridx = lax.broadcasted_iota(jnp.int32, (nc, Q_BLOCK), 0)
    blk = ridx >> 1
    cur = t_b >> 6
    forced = (blk == 0) | (blk == cur) | (blk == cur - 1)
    score = jnp.where(blk <= cur, jnp.where(forced, FORCED_SCORE, imp2), -FORCED_SCORE)
    rank = jnp.zeros((nc, Q_BLOCK), F32)
    for i in range(0, nc, 2):
        s_i = score[i:i + 1, :]
        ahead = (s_i > score) | ((s_i == score) & (ridx > i))
        rank = rank + jnp.where(ahead, 1.0, 0.0)
    sel_t = jnp.where((rank < n_sel) & ((ridx & 1) == 0), 1.0, 0.0).astype(BF16)
    selmask_ref[...] = (_tn(sel_t, e_ref[...]) - 1.0) * (-NEG_INF)

    q_all = jnp.concatenate(qh, axis=0)
    rc = (lax.broadcasted_iota(jnp.int32, (Q_BLOCK, K_BLOCK), 0)
          - lax.broadcasted_iota(jnp.int32, (Q_BLOCK, K_BLOCK), 1))

    def branch(first_kb, col_k, col_v, j, use_sel):
        _reset_state(m_ref, l_ref, acc_ref)

        def body(kb, carry):
            k0 = pl.multiple_of(kb * K_BLOCK, K_BLOCK)
            k = kvb_ref[pl.ds(k0, K_BLOCK), col_k:col_k + NSA_DK]
            v = kvb_ref[pl.ds(k0, K_BLOCK), col_v:col_v + NSA_DK]
            dist_i = rc + (qi * Q_BLOCK - kb * K_BLOCK)
            dist = dist_i.astype(F32)
            if use_sel:
                base = jnp.where(dist_i >= 0, 0.0, NEG_INF) + selmask_ref[:, pl.ds(k0, K_BLOCK)]
            else:
                base = jnp.where(dist_i >= 0, jnp.where(dist_i <= NSA_WINDOW, 0.0, NEG_INF), NEG_INF)
            pen = jnp.concatenate([base - ALIBI[h] * dist for h in range(NSA_HEADS)], axis=0)
            _online_step(_nt(q_all, k) + pen, v, m_ref, l_ref, acc_ref)
            return carry

        lax.fori_loop(first_kb, (qi * Q_BLOCK) // K_BLOCK + 1, body, 0)
        out = acc_ref[...] / jnp.maximum(l_ref[...], 1e-30)
        for h in range(NSA_HEADS):
            o_ref[:, h * NSA_DK:(h + 1) * NSA_DK] += gate(j, h) * out[h * Q_BLOCK:(h + 1) * Q_BLOCK]

    branch(0, 0, NSA_DK, 1, True)
    branch(jnp.maximum(qi * Q_BLOCK - NSA_WINDOW, 0) // K_BLOCK, 2 * NSA_DK, 3 * NSA_DK, 2, False)


def _nsa_prompt(z, kc, vc, kvb, *, batch, seq):
    nq = seq // Q_BLOCK
    nc = seq // NSA_L_CMP
    n_sb = -(-seq // NSA_L_SEL)
    expand = np.zeros((nc, seq), np.float32)
    for j in range(n_sb):
        expand[2 * j, j * NSA_L_SEL:(j + 1) * NSA_L_SEL] = 1.0
    return pl.pallas_call(
        functools.partial(_nsa_prompt_kernel, seq=seq, n_sel=min(NSA_TOPN, n_sb)),
        grid=(batch, nq),
        in_specs=[pl.BlockSpec((Q_BLOCK, 512), lambda b, i: (b * nq + i, Z_NSA_Q // 512)),
                  pl.BlockSpec((Q_BLOCK, LANES), lambda b, i: (b * nq + i, Z_MISC // LANES)),
                  pl.BlockSpec((nc, LANES), lambda b, i: (b, 0)),
                  pl.BlockSpec((nc, LANES), lambda b, i: (b, 0)),
                  pl.BlockSpec((seq, 512), lambda b, i: (b, 0)),
                  pl.BlockSpec((nc, seq), lambda b, i: (0, 0))],
        out_specs=pl.BlockSpec((Q_BLOCK, 512), lambda b, i: (b * nq + i, 0)),
        out_shape=jax.ShapeDtypeStruct((batch * seq, 512), F32),
        scratch_shapes=[pltpu.VMEM((Q_BLOCK, seq), F32)] + [pltpu.VMEM((NSA_HEADS * Q_BLOCK, LANES), F32)] * 3,
        compiler_params=_params(2),
        name="nsa_prompt",
    )(z, z, kc, vc, kvb, jnp.asarray(expand, BF16))


def _silu(x):
    return x * jax.nn.sigmoid(x)


def _mix_and_project(x, y_pool, olat, y_conv, y_nsa, g_pool, g_mla, g_conv, g_nsa, wuv_ref, wout_ref):
    y_mla = jnp.concatenate(
        [_nn(olat[:, h * MLA_KV_LORA:(h + 1) * MLA_KV_LORA].astype(BF16), wuv_ref[h]) for h in range(MLA_HEADS)],
        axis=1)
    mix = jnp.concatenate([y_pool * _silu(g_pool), y_mla * _silu(g_mla),
                           y_conv * _silu(g_conv), y_nsa * _silu(g_nsa)], axis=1)
    return x + _nn(mix.astype(BF16), wout_ref[...])


def _pool_project(d, pw_ref, ps_ref):
    return jnp.concatenate(
        [_nn(d[:, g * POOL_CH:(g + 1) * POOL_CH].astype(BF16), pw_ref[g]) for g in range(len(POOL_WINDOWS))],
        axis=1) * ps_ref[...]


def _out_prompt_kernel(x_ref, u_ref, uh_ref, gp_ref, gm_ref, ch_ref, cc_ref, cb_ref, gc_ref, gn_ref,
                       chh_ref, cch_ref, olat_ref, ynsa_ref, pw_ref, ps_ref, wuv_ref, cw_ref, cbias_ref,
                       wout_ref, fg_ref, *out_refs, tb, seq, final):
    if final:
        xo_ref, y_ref, zc_ref = out_refs
    else:
        xo_ref, zc_ref = out_refs
    i = pl.program_id(0)
    t0 = (i * tb) % seq
    first = t0 == 0
    u = u_ref[...]
    full = jnp.concatenate([jnp.where(first, 0.0, uh_ref[...]), u], axis=0)
    t = (t0 + lax.broadcasted_iota(jnp.int32, (tb, POOL_CH), 0) + 1).astype(F32)
    sums = full
    means = []
    step = 1
    for g, w in enumerate(POOL_WINDOWS):
        while step < w:
            sums = sums + pltpu.roll(sums, step, 0)
            step *= 2
        sg = sums[:, g * POOL_CH:(g + 1) * POOL_CH]
        means.append(sg[POOL_HIST + 1:, :] / jnp.minimum(float(w), t))
    d = jnp.concatenate(means, axis=1) - u
    y_pool = _pool_project(d, pw_ref, ps_ref)
    zc = cc_ref[...] * ch_ref[...]
    zfull = jnp.concatenate([jnp.where(first, 0.0, cch_ref[...] * chh_ref[...]), zc], axis=0)
    cw = cw_ref[...]
    conv = (cw[0:1] * pltpu.roll(zfull, 2, 0)[8:] + cw[1:2] * pltpu.roll(zfull, 1, 0)[8:]
            + cw[2:3] * zc + cbias_ref[...])
    y_conv = cb_ref[...] * conv
    zc_ref[...] = zc[tb - 8:, :]
    out = _mix_and_project(x_ref[...], y_pool, olat_ref[...], y_conv, ynsa_ref[...],
                           gp_ref[...], gm_ref[...], gc_ref[...], gn_ref[...], wuv_ref, wout_ref)
    xo_ref[...] = out
    if final:
        y_ref[...] = _rms(out, fg_ref[...])


def _out_prompt(x, z, olat, ynsa, lw, final_gain, *, seq, tb, final):
    m = x.shape[0]
    zblk = lambda c: pl.BlockSpec((tb, 512), lambda i: (i, c // 512))
    halo16 = pl.BlockSpec((16, 512), lambda i: (jnp.maximum(i * (tb // 16) - 1, 0), Z_POOL_U // 512))
    halo8 = lambda c: pl.BlockSpec((8, 512), lambda i: (jnp.maximum(i * (tb // 8) - 1, 0), c // 512))
    row = lambda w: pl.BlockSpec((tb, w), lambda i: (i, 0))
    cst2 = lambda a, b: pl.BlockSpec((a, b), lambda i: (0, 0))
    cst3 = lambda a, b, c: pl.BlockSpec((a, b, c), lambda i: (0, 0, 0))
    in_specs = [row(D_MODEL), zblk(Z_POOL_U), halo16, zblk(Z_POOL_G), zblk(Z_MLA_G),
                zblk(Z_CONV_H), zblk(Z_CONV_C), zblk(Z_CONV_B), zblk(Z_CONV_G), zblk(Z_NSA_G),
                halo8(Z_CONV_H), halo8(Z_CONV_C), row(512), row(512),
                cst3(len(POOL_WINDOWS), POOL_CH, POOL_CH), cst2(1, D_GROUP),
                cst3(MLA_HEADS, MLA_KV_LORA, 128), cst2(CONV_WIDTH, D_GROUP), cst2(1, D_GROUP),
                cst2(D_MODEL, D_MODEL), cst2(1, D_MODEL)]
    out_specs = [row(D_MODEL)] + ([row(D_MODEL)] if final else []) + [pl.BlockSpec((8, 512), lambda i: (i, 0))]
    out_shape = ([jax.ShapeDtypeStruct((m, D_MODEL), F32)] * (2 if final else 1)
                 + [jax.ShapeDtypeStruct((m // tb * 8, 512), F32)])
    return pl.pallas_call(
        functools.partial(_out_prompt_kernel, tb=tb, seq=seq, final=final),
        grid=(m // tb,),
        in_specs=in_specs, out_specs=out_specs, out_shape=out_shape,
        compiler_params=_params(1),
        name="out_prompt_final" if final else "out_prompt",
    )(x, z, z, z, z, z, z, z, z, z, z, z, olat, ynsa, lw["pool_w"], lw["pool_scale"], lw["w_uv"],
      lw["conv_w"], lw["conv_b"], lw["w_out"], final_gain)


def _out_sample_kernel(x_ref, z_ref, pst_ref, c0_ref, c1_ref, olat_ref, ynsa_ref, pw_ref, ps_ref, wuv_ref,
                       cw_ref, cbias_ref, wout_ref, fg_ref, *out_refs, n_past, final):
    if final:
        xo_ref, y_ref, zc_ref = out_refs
    else:
        xo_ref, zc_ref = out_refs
    zcol = lambda c: z_ref[:, c:c + 512]
    u = zcol(Z_POOL_U)
    sums = u
    means = []
    back = 1
    for g, w in enumerate(POOL_WINDOWS):
        while back < w:
            sums = sums + pst_ref[POOL_HIST - back]
            back += 1
        means.append(sums[:, g * POOL_CH:(g + 1) * POOL_CH] / float(min(w, n_past + 1)))
    d = jnp.concatenate(means, axis=1) - u
    y_pool = _pool_project(d, pw_ref, ps_ref)
    zc = zcol(Z_CONV_C) * zcol(Z_CONV_H)
    cw = cw_ref[...]
    conv = cw[0:1] * c0_ref[...] + cw[1:2] * c1_ref[...] + cw[2:3] * zc + cbias_ref[...]
    y_conv = zcol(Z_CONV_B) * conv
    zc_ref[...] = zc
    out = _mix_and_project(x_ref[...], y_pool, olat_ref[...], y_conv, ynsa_ref[...],
                           zcol(Z_POOL_G), zcol(Z_MLA_G), zcol(Z_CONV_G), zcol(Z_NSA_G), wuv_ref, wout_ref)
    xo_ref[...] = out
    if final:
        y_ref[...] = _rms(out, fg_ref[...])


def _out_sample(x, z, pool_state_t, conv0, conv1, olat, ynsa, lw, final_gain, *, n_past, final):
    m = x.shape[0]
    n_out = 3 if final else 2
    out_shape = ([jax.ShapeDtypeStruct((m, D_MODEL), F32)] * (n_out - 1) + [jax.ShapeDtypeStruct((m, 512), F32)])
    return pl.pallas_call(
        functools.partial(_out_sample_kernel, n_past=n_past, final=final),
        out_shape=out_shape,
        compiler_params=pltpu.CompilerParams(vmem_limit_bytes=VMEM_LIMIT),
        name="out_sample_final" if final else "out_sample",
    )(x, z, pool_state_t, conv0, conv1, olat, ynsa, lw["pool_w"], lw["pool_scale"], lw["w_uv"],
      lw["conv_w"], lw["conv_b"], lw["w_out"], final_gain)


HEAD_ROWS = 16


def _rows_to_row(o):
    return jnp.concatenate([o[h:h + 1, :] for h in range(NSA_HEADS)], axis=1)


def _sample_paged_kernel(pt_ref, lat_hbm, kr_hbm, ck_hbm, cv_hbm, qm_ref, qr_ref, qn_ref, latn_ref, krn_ref,
                         gk_ref, gv_ref, olat_ref, ocmp_ref, imp_ref,
                         lat_buf, kr_buf, ck_buf, cv_buf, sem, m_ref, l_ref, acc_ref, kc_ref, vc_ref,
                         *, layer, cp, n_chunks, n_past, batch):
    b = pl.program_id(0)
    krows = MLA_ROPE

    def copies(row, chunk, slot):
        out = []
        for i in range(cp):
            page = pt_ref[row, chunk * cp + i]
            out.append(pltpu.make_async_copy(lat_hbm.at[layer, page], lat_buf.at[slot, pl.ds(i * PAGE_SIZE, PAGE_SIZE)],
                                             sem.at[slot]))
            out.append(pltpu.make_async_copy(kr_hbm.at[layer, page], kr_buf.at[slot, pl.ds(i * krows, krows)],
                                             sem.at[slot]))
            out.append(pltpu.make_async_copy(ck_hbm.at[layer, page], ck_buf.at[slot, pl.ds(i * PAGE_SIZE, PAGE_SIZE)],
                                             sem.at[slot]))
            out.append(pltpu.make_async_copy(cv_hbm.at[layer, page], cv_buf.at[slot, pl.ds(i * PAGE_SIZE, PAGE_SIZE)],
                                             sem.at[slot]))
        return out

    @pl.when(b == 0)
    def _():
        for cpy in copies(0, 0, 0):
            cpy.start()

    m_ref[...] = jnp.full(m_ref.shape, M_INIT, F32)
    l_ref[...] = jnp.zeros(l_ref.shape, F32)
    acc_ref[...] = jnp.zeros(acc_ref.shape, F32)
    qm = qm_ref[...]
    qr = qr_ref[:, :MLA_ROPE]
    gk = gk_ref[...][None]
    gv = gv_ref[...][None]
    bpc = cp * PAGE_SIZE // NSA_L_CMP

    def chunk_body(c, carry):
        g = b * n_chunks + c
        slot = g % 2
        last = c == n_chunks - 1
        nrow = jnp.where(last, b + 1, b)
        nchunk = jnp.where(last, 0, c + 1)

        @pl.when(g + 1 < batch * n_chunks)
        def _():
            for cpy in copies(nrow, nchunk, 1 - slot):
                cpy.start()

        for cpy in copies(b, c, slot):
            cpy.wait()

        lat = lat_buf[slot].astype(BF16)
        kr = kr_buf[slot].astype(BF16)
        s_rope = jnp.concatenate([_nn(qr, kr[i * krows:(i + 1) * krows, :]) for i in range(cp)], axis=1)
        s = _nt(qm, lat) + s_rope
        m_prev = m_ref[...]
        m_new = jnp.maximum(m_prev, jnp.max(s, axis=1, keepdims=True))
        alpha = jnp.exp(m_prev - m_new)
        p = jnp.exp(s - m_new[:, 0:1])
        l_ref[...] = alpha * l_ref[...] + jnp.sum(p, axis=1, keepdims=True)
        acc_ref[...] = alpha * acc_ref[...] + _nn(p.astype(BF16), lat)
        m_ref[...] = m_new
        r0 = pl.multiple_of(c * bpc, bpc)
        kc_ref[pl.ds(r0, bpc), :] = jnp.sum(ck_buf[slot].reshape(bpc, NSA_L_CMP, NSA_DK) * gk, axis=1) * (1.0 / NSA_L_CMP)
        vc_ref[pl.ds(r0, bpc), :] = jnp.sum(cv_buf[slot].reshape(bpc, NSA_L_CMP, NSA_DK) * gv, axis=1) * (1.0 / NSA_L_CMP)
        return carry

    lax.fori_loop(0, n_chunks, chunk_body, 0)

    latn = latn_ref[...].astype(BF16)
    is_new = lax.broadcasted_iota(jnp.int32, (HEAD_ROWS, HEAD_ROWS), 1) == 0
    s_n = jnp.where(is_new, _nt(qm, latn) + _nt(qr, krn_ref[:, :MLA_ROPE].astype(BF16)), NEG_INF)
    m_prev = m_ref[...]
    m_f = jnp.maximum(m_prev, jnp.max(s_n, axis=1, keepdims=True))
    a_f = jnp.exp(m_prev - m_f)
    p_n = jnp.where(is_new, jnp.exp(s_n - m_f[:, 0:HEAD_ROWS]), 0.0)
    l_f = l_ref[...] * a_f + jnp.sum(p_n, axis=1, keepdims=True)
    acc_f = acc_ref[...] * a_f + _nn(p_n.astype(BF16), latn)
    olat_ref[...] = _rows_to_row(acc_f / jnp.maximum(l_f, 1e-30))

    nblk = n_past // NSA_L_CMP
    s = _nt(qn_ref[...], kc_ref[...].astype(BF16))
    hrow = lax.broadcasted_iota(jnp.int32, (HEAD_ROWS, nblk), 0)
    bend = lax.broadcasted_iota(jnp.int32, (HEAD_ROWS, nblk), 1) * NSA_L_CMP + (NSA_L_CMP - 1)
    slope = functools.reduce(lambda a, h: jnp.where(hrow == h, ALIBI[h], a), range(NSA_HEADS),
                             jnp.zeros((HEAD_ROWS, nblk), F32))
    valid = bend <= n_past
    s = jnp.where(valid, s - slope * (n_past - bend).astype(F32), NEG_INF)
    e = jnp.where(valid, jnp.exp(s - jnp.max(s, axis=1, keepdims=True)), 0.0)
    p = e / jnp.maximum(jnp.sum(e, axis=1, keepdims=True), 1e-30)
    imp_ref[...] = jnp.sum(jnp.where(hrow < NSA_HEADS, p, 0.0), axis=0, keepdims=True)
    ocmp_ref[...] = _rows_to_row(_nn(p.astype(BF16), vc_ref[...].astype(BF16)))


def _sample_paged(page_table, c_lat, c_kr_t, c_ck, c_cv, qm, qr, qn, latn, krn, gk, gv, *, layer, cp):
    batch, n_pages = page_table.shape
    n_chunks = n_pages // cp
    n_past = n_pages * PAGE_SIZE
    nblk = n_past // NSA_L_CMP
    hbm = pl.BlockSpec(memory_space=pl.ANY)
    per_b3 = lambda r, w: pl.BlockSpec((None, r, w), lambda b, pt: (b, 0, 0))
    cst2 = lambda a, w: pl.BlockSpec((a, w), lambda b, pt: (0, 0))
    in_specs = [hbm, hbm, hbm, hbm,
                per_b3(HEAD_ROWS, LANES), per_b3(HEAD_ROWS, LANES), per_b3(HEAD_ROWS, LANES),
                per_b3(HEAD_ROWS, LANES), per_b3(HEAD_ROWS, LANES), cst2(NSA_L_CMP, NSA_DK), cst2(NSA_L_CMP, NSA_DK)]
    out_specs = [per_b3(1, 512), per_b3(1, 512), per_b3(1, nblk)]
    out_shape = [jax.ShapeDtypeStruct((batch, 1, 512), F32), jax.ShapeDtypeStruct((batch, 1, 512), F32),
                 jax.ShapeDtypeStruct((batch, 1, nblk), F32)]
    grid_spec = pltpu.PrefetchScalarGridSpec(
        num_scalar_prefetch=1, grid=(batch,), in_specs=in_specs, out_specs=out_specs,
        scratch_shapes=[pltpu.VMEM((2, cp * PAGE_SIZE, MLA_KV_LORA), F32),
                        pltpu.VMEM((2, cp * MLA_ROPE, PAGE_SIZE), F32),
                        pltpu.VMEM((2, cp * PAGE_SIZE, NSA_DK), F32),
                        pltpu.VMEM((2, cp * PAGE_SIZE, NSA_DK), F32),
                        pltpu.SemaphoreType.DMA((2,)),
                        pltpu.VMEM((HEAD_ROWS, LANES), F32), pltpu.VMEM((HEAD_ROWS, LANES), F32),
                        pltpu.VMEM((HEAD_ROWS, MLA_KV_LORA), F32),
                        pltpu.VMEM((nblk, NSA_DK), F32), pltpu.VMEM((nblk, NSA_DK), F32)])
    return pl.pallas_call(
        functools.partial(_sample_paged_kernel, layer=layer, cp=cp, n_chunks=n_chunks, n_past=n_past, batch=batch),
        grid_spec=grid_spec, out_shape=out_shape,
        compiler_params=_params(1),
        name="sample_paged",
    )(page_table, c_lat, c_kr_t, c_ck, c_cv, qm, qr, qn, latn, krn, gk, gv)


def _select_kernel(imp_ref, idx_ref, *, n_past, n_sel):
    rows, nblk = imp_ref.shape
    ratio = NSA_L_SEL // NSA_L_CMP
    n_sb = -(-(n_past + 1) // NSA_L_SEL)
    n_in = nblk // ratio
    cur = n_past // NSA_L_SEL
    x = imp_ref[...]
    pair = x + pltpu.roll(x, nblk - 1, 1)
    lane = lax.broadcasted_iota(jnp.int32, (rows, nblk + LANES), 1)
    blk = jnp.where(lane < nblk, lane >> 1, n_in + lane - nblk)
    exists = ((lane < nblk) & ((lane & 1) == 0)) | ((lane >= nblk) & (blk < n_sb))
    imp = jnp.concatenate([pair, jnp.zeros((rows, LANES), F32)], axis=1)
    forced = (blk == 0) | (blk == cur) | (blk == cur - 1)
    score = jnp.where(blk * NSA_L_SEL <= n_past, jnp.where(forced, FORCED_SCORE, imp), -FORCED_SCORE)
    score = jnp.where(exists, score, -3e38)
    lane_f = lane.astype(F32)
    out_lane = lax.broadcasted_iota(jnp.int32, (rows, LANES), 1)
    out = jnp.zeros((rows, LANES), jnp.int32)
    for r in range(n_sel):
        top = jnp.max(score, axis=1, keepdims=True)
        first = jnp.min(jnp.where(score == top, lane_f, 1e9), axis=1, keepdims=True)
        first_i = first.astype(jnp.int32)
        chosen = jnp.where(first_i < nblk, first_i >> 1, n_in + first_i - nblk)
        out = jnp.where(out_lane == r, chosen, out)
        score = jnp.where(lane_f == first, -3e38, score)
    idx_ref[...] = out


def _select(imp, *, n_past, n_sel):
    rows = imp.shape[0]
    return pl.pallas_call(
        functools.partial(_select_kernel, n_past=n_past, n_sel=n_sel),
        out_shape=jax.ShapeDtypeStruct((rows, LANES), jnp.int32),
        compiler_params=pltpu.CompilerParams(vmem_limit_bytes=VMEM_LIMIT),
        name="select",
    )(imp)


def _sample_nsa_kernel(ph_ref, hf_ref, blk_ref, sk_hbm, sv_hbm, qn_ref, ksn_ref, vsn_ref, wk_ref, wv_ref,
                       kwn_ref, vwn_ref, misc_ref, ocmp_ref, o_ref, k_buf, v_buf, sem,
                       *, layer, n_sel, n_past, wb, batch):
    b = pl.program_id(0)

    def copies(row, slot):
        out = []
        for i in range(n_sel):
            src = pl.ds(pl.multiple_of(hf_ref[row, i] * NSA_L_SEL, NSA_L_SEL), NSA_L_SEL)
            dst = pl.ds(i * NSA_L_SEL, NSA_L_SEL)
            out.append(pltpu.make_async_copy(sk_hbm.at[layer, ph_ref[row, i], src], k_buf.at[slot, dst], sem.at[slot]))
            out.append(pltpu.make_async_copy(sv_hbm.at[layer, ph_ref[row, i], src], v_buf.at[slot, dst], sem.at[slot]))
        return out

    @pl.when(b == 0)
    def _():
        for cpy in copies(0, 0):
            cpy.start()

    slot = b % 2

    @pl.when(b + 1 < batch)
    def _():
        for cpy in copies(b + 1, 1 - slot):
            cpy.start()

    for cpy in copies(b, slot):
        cpy.wait()
    k_blocks = [k_buf[slot, i * NSA_L_SEL:(i + 1) * NSA_L_SEL, :] for i in range(n_sel)]
    v_blocks = [v_buf[slot, i * NSA_L_SEL:(i + 1) * NSA_L_SEL, :] for i in range(n_sel)]
    qn = qn_ref[...]
    hcol = lax.broadcasted_iota(jnp.int32, (HEAD_ROWS, 1), 0)
    slope = functools.reduce(lambda a, h: jnp.where(hcol == h, ALIBI[h], a), range(NSA_HEADS),
                             jnp.zeros((HEAD_ROWS, 1), F32))
    is_new = lax.broadcasted_iota(jnp.int32, (HEAD_ROWS, HEAD_ROWS), 1) == 0

    def attend(keys, values, dists, k_new, v_new):
        scores = []
        for k, d in zip(keys, dists):
            scores.append(jnp.where(d >= 0, _nt(qn, k.astype(BF16)) - slope * d.astype(F32), NEG_INF))
        k_new = k_new.astype(BF16)
        s_n = jnp.where(is_new, _nt(qn, k_new), NEG_INF)
        m = functools.reduce(jnp.maximum, [jnp.max(s, axis=1, keepdims=True) for s in scores + [s_n]])
        p_n = jnp.where(is_new, jnp.exp(s_n - m), 0.0)
        lsum = jnp.sum(p_n, axis=1, keepdims=True)
        acc = _nn(p_n.astype(BF16), v_new.astype(BF16))
        for s, v, d in zip(scores, values, dists):
            p = jnp.where(d >= 0, jnp.exp(s - m), 0.0)
            lsum = lsum + jnp.sum(p, axis=1, keepdims=True)
            acc = acc + _nn(p.astype(BF16), v.astype(BF16))
        return _rows_to_row(acc / jnp.maximum(lsum, 1e-30))

    off = lax.broadcasted_iota(jnp.int32, (1, NSA_L_SEL), 1)
    dists = []
    for i in range(n_sel):
        blk = blk_ref[b, i]
        dists.append(jnp.where(blk < n_past // NSA_L_SEL, n_past - blk * NSA_L_SEL, -1) - off)
    o_sel = attend(k_blocks, v_blocks, dists, ksn_ref[...], vsn_ref[...])
    wd = wb - lax.broadcasted_iota(jnp.int32, (1, wb), 1)
    wd = jnp.where(wd <= NSA_WINDOW, wd, -1)
    o_win = attend([wk_ref[...]], [wv_ref[...]], [wd], kwn_ref[...], vwn_ref[...])
    gates = jax.nn.sigmoid(misc_ref[...])
    o_cmp = ocmp_ref[...]
    pieces = []
    for h in range(NSA_HEADS):
        sl = slice(h * NSA_DK, (h + 1) * NSA_DK)
        g = [gates[:, MISC_GATE0 + NSA_HEADS * j + h:MISC_GATE0 + NSA_HEADS * j + h + 1] for j in range(3)]
        pieces.append(g[0] * o_cmp[:, sl] + g[1] * o_sel[:, sl] + g[2] * o_win[:, sl])
    o_ref[...] = jnp.concatenate(pieces, axis=1)


def _sample_nsa(phys, half, blk, c_sk, c_sv, qn, ksn, vsn, win_k, win_v, kwn, vwn, misc, ocmp, *,
                layer, n_past):
    batch, n_sel = blk.shape
    wb = win_k.shape[2]

    hbm = pl.BlockSpec(memory_space=pl.ANY)
    per_b3 = lambda r, w: pl.BlockSpec((None, r, w), lambda b, ph, hf, bl: (b, 0, 0))
    win = pl.BlockSpec((None, None, wb, NSA_DK), lambda b, ph, hf, bl: (layer, b, 0, 0))
    in_specs = [hbm, hbm, per_b3(HEAD_ROWS, LANES), per_b3(HEAD_ROWS, LANES), per_b3(HEAD_ROWS, LANES), win, win,
                per_b3(HEAD_ROWS, LANES), per_b3(HEAD_ROWS, LANES), per_b3(1, LANES), per_b3(1, 512)]
    grid_spec = pltpu.PrefetchScalarGridSpec(
        num_scalar_prefetch=3, grid=(batch,), in_specs=in_specs, out_specs=per_b3(1, 512),
        scratch_shapes=[pltpu.VMEM((2, n_sel * NSA_L_SEL, NSA_DK), F32),
                        pltpu.VMEM((2, n_sel * NSA_L_SEL, NSA_DK), F32),
                        pltpu.SemaphoreType.DMA((2,))])
    return pl.pallas_call(
        functools.partial(_sample_nsa_kernel, layer=layer, n_sel=n_sel, n_past=n_past, wb=wb, batch=batch),
        grid_spec=grid_spec, out_shape=jax.ShapeDtypeStruct((batch, 1, 512), F32),
        compiler_params=_params(1),
        name="sample_nsa",
    )(phys, half, blk, c_sk, c_sv, qn, ksn, vsn, win_k, win_v, kwn, vwn, misc, ocmp)


def _rope_tables(pos):
    half = MLA_ROPE // 2
    inv = ROPE_THETA ** (-jnp.arange(half, dtype=F32) / half)
    ang = pos.astype(F32)[:, None] * inv
    cos, sin = jnp.cos(ang), jnp.sin(ang)
    reps = LANES // MLA_ROPE
    return (jnp.tile(jnp.concatenate([cos, cos], axis=1), (1, reps)),
            jnp.tile(jnp.concatenate([-sin, sin], axis=1), (1, reps)))


def _layer_weights(l, norm_gain, w_in, w_out, pool_w, pool_scale, mla_q_norm, mla_kv_norm, mla_w_uq, mla_w_uk,
                   mla_w_uv, conv_w, conv_b, nsa_cmp_pos_k, nsa_cmp_pos_v):
    w = w_in[l]
    o_kr, o_mg, o_q, o_kc, o_bg, o_ng, o_end = 1536, 1568, 4128, 4640, 5408, 5420, 5932
    w_p = jnp.concatenate(
        [w[:, :o_kr], w[:, o_mg:o_q + 512], w[:, o_ng:o_end], w[:, o_kc:o_bg], w[:, o_kr:o_mg],
         w[:, o_bg:o_ng], jnp.zeros((D_MODEL, Z_WIDTH - Z_MISC - MLA_ROPE - 3 * NSA_HEADS), w.dtype)],
        axis=1).astype(BF16)
    uq = mla_w_uq[l].reshape(MLA_Q_LORA, MLA_HEADS, MLA_NOPE + MLA_ROPE)
    return {
        "norm": norm_gain[l][None], "w_in": w_p, "w_out": w_out[l].astype(BF16),
        "pool_w": pool_w[l].astype(BF16), "pool_scale": pool_scale[l][None],
        "q_norm": mla_q_norm[l][None], "kv_norm": mla_kv_norm[l][None],
        "w_uq_nope": uq[:, :, :MLA_NOPE].reshape(MLA_Q_LORA, -1).astype(BF16),
        "w_uq_rope": uq[:, :, MLA_NOPE:].reshape(MLA_Q_LORA, -1).astype(BF16),
        "w_ukT": jnp.transpose(mla_w_uk[l], (1, 2, 0)).astype(BF16),
        "w_uv": jnp.transpose(mla_w_uv[l], (1, 0, 2)).astype(BF16),
        "conv_w": conv_w[l], "conv_b": conv_b[l][None],
        "gain_k": nsa_cmp_pos_k[l], "gain_v": nsa_cmp_pos_v[l],
    }


def _pad_rows(a, rows):
    return jnp.pad(a, ((0, 0), (0, rows - a.shape[1]), (0, 0)))


def kernel(x_prompt, x_sample, cache_mla_latent, cache_mla_krope, cache_nsa_cmp_k, cache_nsa_cmp_v,
           cache_nsa_sel_k, cache_nsa_sel_v, state_nsa_win_k, state_nsa_win_v, state_conv, state_pool,
           page_table, norm_gain, w_in, w_out, pool_w, pool_scale, mla_q_norm, mla_kv_norm,
           mla_w_uq, mla_w_uk, mla_w_uv, conv_w, conv_b, nsa_cmp_pos_k, nsa_cmp_pos_v, final_norm):
    batch, seq, _ = x_prompt.shape
    dec, dec_seq, _ = x_sample.shape
    depth = norm_gain.shape[0]
    n_pages = page_table.shape[1]
    n_past = n_pages * PAGE_SIZE
    wb = state_nsa_win_k.shape[2]
    assert dec_seq == 1 and seq % 256 == 0 and dec % 8 == 0
    n_sel = min(NSA_TOPN, -(-(n_past + dec_seq) // NSA_L_SEL))
    cp = 16 if n_pages % 32 == 0 else (8 if n_pages % 8 == 0 else n_pages)
    cache_krope_t = jnp.swapaxes(cache_mla_krope, 2, 3)

    cos_p, sin_p = _rope_tables(jnp.arange(seq, dtype=jnp.int32))
    cos_s, sin_s = _rope_tables(jnp.full((dec,), n_past, jnp.int32))
    final_gain = final_norm[None]

    xp = x_prompt.reshape(batch * seq, D_MODEL)
    xs = x_sample.reshape(dec, D_MODEL)
    new_p, new_s = [], []
    yp = ys = None
    for l in range(depth):
        final = l == depth - 1
        lw = _layer_weights(l, norm_gain, w_in, w_out, pool_w, pool_scale, mla_q_norm, mla_kv_norm, mla_w_uq,
                            mla_w_uk, mla_w_uv, conv_w, conv_b, nsa_cmp_pos_k, nsa_cmp_pos_v)
        zp = _inproj(xp, lw["norm"], lw["w_in"], tm=1024, tn=1536)
        qcat, kcat, lat, kr, kvb, kc, vc = _prep(zp, cos_p, sin_p, lw, tb=256, rows_per_table=seq, with_nsa=True)
        olat = _mla_prompt(qcat, kcat, batch=batch, seq=seq)
        ynsa = _nsa_prompt(zp, kc, vc, kvb, batch=batch, seq=seq)
        outs = _out_prompt(xp, zp, olat, ynsa, lw, final_gain, seq=seq, tb=256, final=final)
        if final:
            xp, yp, zc_tail = outs
        else:
            xp, zc_tail = outs
        z3 = zp.reshape(batch, seq, Z_WIDTH)
        wbp = min(NSA_WINDOW, seq)
        zc_last = zc_tail.reshape(batch, seq // 256, 8, 512)[:, -1]
        new_p.append((lat.reshape(batch, seq, LANES), kr.reshape(batch, seq, LANES)[:, :, :MLA_ROPE],
                      z3[:, :, Z_KC:Z_KC + NSA_DK], z3[:, :, Z_VC:Z_VC + NSA_DK],
                      z3[:, :, Z_KS:Z_KS + NSA_DK], z3[:, :, Z_VS:Z_VS + NSA_DK],
                      z3[:, seq - wbp:, Z_KW:Z_KW + NSA_DK], z3[:, seq - wbp:, Z_VW:Z_VW + NSA_DK],
                      zc_last[:, 8 - (CONV_WIDTH - 1):], z3[:, seq - POOL_HIST:, Z_POOL_U:Z_POOL_U + D_GROUP]))
        zs = _inproj(xs, lw["norm"], lw["w_in"], tm=dec, tn=1536)
        qcat_s, _, lat_s, kr_s = _prep(zs, cos_s, sin_s, lw, tb=dec, rows_per_table=dec, with_nsa=False)
        q4 = qcat_s.reshape(dec, MLA_HEADS, 256)
        qm = _pad_rows(q4[:, :, :LANES], HEAD_ROWS)
        qr = _pad_rows(q4[:, :, LANES:], HEAD_ROWS)
        qn = _pad_rows((zs[:, Z_NSA_Q:Z_NSA_Q + D_GROUP] * NSA_SCALE).astype(BF16).reshape(dec, NSA_HEADS, NSA_DK),
                       HEAD_ROWS)
        new_row = lambda a: _pad_rows(a[:, None, :], HEAD_ROWS)
        olat_s, ocmp_s, imp = _sample_paged(
            page_table, cache_mla_latent, cache_krope_t, cache_nsa_cmp_k, cache_nsa_cmp_v,
            qm, qr, qn, new_row(lat_s), new_row(kr_s), lw["gain_k"], lw["gain_v"], layer=l, cp=cp)
        blk = _select(imp.reshape(dec, -1), n_past=n_past, n_sel=n_sel)[:, :n_sel]
        page = jnp.minimum(blk // (PAGE_SIZE // NSA_L_SEL), n_pages - 1)
        phys = jnp.take_along_axis(page_table, page, axis=1)
        half = blk % (PAGE_SIZE // NSA_L_SEL)
        zcol = lambda c, w=NSA_DK: zs[:, c:c + w]
        ynsa_s = _sample_nsa(
            phys, half, blk, cache_nsa_sel_k, cache_nsa_sel_v, qn, new_row(zcol(Z_KS)), new_row(zcol(Z_VS)),
            state_nsa_win_k, state_nsa_win_v, new_row(zcol(Z_KW)), new_row(zcol(Z_VW)),
            zcol(Z_MISC, LANES)[:, None, :], ocmp_s, layer=l, n_past=n_past)
        outs = _out_sample(xs, zs, jnp.swapaxes(state_pool[l], 0, 1), state_conv[l, :, 0], state_conv[l, :, 1],
                           olat_s.reshape(dec, 512), ynsa_s.reshape(dec, 512), lw, final_gain,
                           n_past=n_past, final=final)
        if final:
            xs, ys, zc_s = outs
        else:
            xs, zc_s = outs
        app = lambda old, new: jnp.concatenate([old, new[:, None, :]], axis=1)[:, -old.shape[1]:]
        new_s.append((lat_s[:, None, :], kr_s[:, None, :MLA_ROPE],
                      zcol(Z_KC)[:, None], zcol(Z_VC)[:, None], zcol(Z_KS)[:, None], zcol(Z_VS)[:, None],
                      app(state_nsa_win_k[l], zcol(Z_KW)), app(state_nsa_win_v[l], zcol(Z_VW)),
                      app(state_conv[l], zc_s), app(state_pool[l], zcol(Z_POOL_U, D_GROUP))))
    stack = lambda rows: tuple(jnp.stack([r[i] for r in rows]) for i in range(len(rows[0])))
    return ((yp.reshape(batch, seq, D_MODEL), ys.reshape(dec, 1, D_MODEL)) + stack(new_p) + stack(new_s))
```

```python
import functools

import numpy as np
import jax
import jax.numpy as jnp
from jax import lax
from jax.experimental import pallas as pl
from jax.experimental.pallas import tpu as pltpu

F32 = jnp.float32
BF16 = jnp.bfloat16

D_MODEL = 2048
D_GROUP = 512
POOL_WINDOWS = (2, 4, 8, 16)
POOL_CH = 128
POOL_HIST = 15
MLA_HEADS = 4
MLA_NOPE = 128
MLA_ROPE = 32
MLA_Q_LORA = 384
MLA_KV_LORA = 128
ROPE_THETA = 10000.0
MLA_SCALE = (MLA_NOPE + MLA_ROPE) ** -0.5
CONV_WIDTH = 3
NSA_HEADS = 4
NSA_DK = 128
NSA_L_CMP = 32
NSA_L_SEL = 64
NSA_TOPN = 16
NSA_WINDOW = 512
NSA_SCALE = NSA_DK ** -0.5
FORCED_SCORE = 1e6
RMS_EPS = 1e-6
NEG_INF = -1e30
M_INIT = -5e29
PAGE_SIZE = 128
ALIBI = tuple(2.0 ** (-8.0 * (h + 1) / NSA_HEADS) for h in range(NSA_HEADS))

LANES = 128
Q_BLOCK = 128
K_BLOCK = 256
VMEM_LIMIT = 56 * 1024 * 1024

Z_POOL_U, Z_POOL_G = 0, 512
Z_CQ, Z_CKV = 1024, 1408
Z_MLA_G = 1536
Z_CONV_H, Z_CONV_C, Z_CONV_B, Z_CONV_G = 2048, 2560, 3072, 3584
Z_NSA_Q, Z_NSA_G = 4096, 4608
Z_KC, Z_VC, Z_KS, Z_VS, Z_KW, Z_VW = 5120, 5248, 5376, 5504, 5632, 5760
Z_MISC = 5888
Z_WIDTH = 6144
MISC_GATE0 = 32


def _nn(a, b):
    return jnp.dot(a, b, preferred_element_type=F32)


def _nt(a, b):
    return lax.dot_general(a, b, (((1,), (1,)), ((), ())), preferred_element_type=F32)


def _tn(a, b):
    return lax.dot_general(a, b, (((0,), (0,)), ((), ())), preferred_element_type=F32)


def _rms(x, g):
    return x * lax.rsqrt(jnp.mean(x * x, axis=-1, keepdims=True) + RMS_EPS) * g


def _params(n_axes):
    return pltpu.CompilerParams(dimension_semantics=("arbitrary",) * n_axes,
                                vmem_limit_bytes=VMEM_LIMIT)


W_KROPE, W_MLA_G, W_KC, W_BG, W_NSA_G, W_END = 1536, 1568, 4640, 5408, 5420, 5932


def _win_layout_kernel(w_ref, o_ref):
    x = w_ref[...]
    rows = x.shape[0]
    seg = lambda a, b: x[:, a:b].astype(BF16)
    o_ref[:, :Z_MLA_G] = seg(0, W_KROPE)
    o_ref[:, Z_MLA_G:Z_NSA_G] = seg(W_MLA_G, W_KC)
    o_ref[:, Z_NSA_G:Z_KC] = seg(W_NSA_G, W_END)
    o_ref[:, Z_KC:Z_MISC] = seg(W_KC, W_BG)
    n_gate = W_NSA_G - W_BG
    o_ref[:, Z_MISC:Z_MISC + LANES] = jnp.concatenate(
        [seg(W_KROPE, W_MLA_G), seg(W_BG, W_NSA_G), jnp.zeros((rows, LANES - MLA_ROPE - n_gate), BF16)], axis=1)
    o_ref[:, Z_MISC + LANES:] = jnp.zeros((rows, Z_WIDTH - Z_MISC - LANES), BF16)


def _win_layout(w_in, layer, *, tr):
    _, d, n = w_in.shape
    return pl.pallas_call(
        _win_layout_kernel,
        grid=(d // tr,),
        in_specs=[pl.BlockSpec((None, tr, n), lambda i: (layer, i, 0))],
        out_specs=pl.BlockSpec((tr, Z_WIDTH), lambda i: (i, 0)),
        out_shape=jax.ShapeDtypeStruct((d, Z_WIDTH), BF16),
        compiler_params=_params(1),
        name="win_layout",
    )(w_in)


def _inproj_kernel(x_ref, g_ref, w_ref, o_ref, xn_ref, *, tm):
    @pl.when(pl.program_id(1) == 0)
    def _():
        rows = 64 if tm % 64 == 0 else tm

        def body(i, c):
            r = pl.multiple_of(i * rows, rows)
            xn_ref[pl.ds(r, rows), :] = _rms(x_ref[pl.ds(r, rows), :], g_ref[...]).astype(BF16)
            return c
        lax.fori_loop(0, tm // rows, body, 0)

    o_ref[...] = _nn(xn_ref[...], w_ref[...])


def _inproj(x, g, w, *, tm, tn):
    m = x.shape[0]
    return pl.pallas_call(
        functools.partial(_inproj_kernel, tm=tm),
        grid=(m // tm, Z_WIDTH // tn),
        in_specs=[pl.BlockSpec((tm, D_MODEL), lambda i, j: (i, 0)),
                  pl.BlockSpec((1, D_MODEL), lambda i, j: (0, 0)),
                  pl.BlockSpec((D_MODEL, tn), lambda i, j: (0, j))],
        out_specs=pl.BlockSpec((tm, tn), lambda i, j: (i, j)),
        out_shape=jax.ShapeDtypeStruct((m, Z_WIDTH), F32),
        scratch_shapes=[pltpu.VMEM((tm, D_MODEL), BF16)],
        compiler_params=_params(2),
        name="inproj",
    )(x, g, w)


def _rope128(x, cos, sin, lane):
    sw = jnp.where((lane & (MLA_ROPE - 1)) < MLA_ROPE // 2, pltpu.roll(x, LANES - MLA_ROPE // 2, 1),
                   pltpu.roll(x, MLA_ROPE // 2, 1))
    return x * cos + sw * sin


def _prep_kernel(zq_ref, zn_ref, cos_ref, sin_ref, qg_ref, kg_ref, wqn_ref, wqr_ref, wuk_ref,
                 gk_ref, gv_ref, *out_refs, tb, with_nsa):
    if with_nsa:
        qcat_ref, kcat_ref, lat_ref, kr_ref, kvb_ref, kc_ref, vc_ref = out_refs
    else:
        qcat_ref, kcat_ref, lat_ref, kr_ref = out_refs
    zq = zq_ref[...]
    cos = cos_ref[...]
    sin = sin_ref[...]
    lane = lax.broadcasted_iota(jnp.int32, (tb, LANES), 1)
    cqn = _rms(zq[:, :MLA_Q_LORA], qg_ref[...]).astype(BF16)
    lat = _rms(zq[:, MLA_Q_LORA:], kg_ref[...])
    q_nope = _nn(cqn, wqn_ref[...])
    q_rope = _rope128(_nn(cqn, wqr_ref[...]), cos, sin, lane) * MLA_SCALE
    misc = zn_ref[:, Z_MISC - Z_KC:Z_MISC - Z_KC + LANES]
    kr = jnp.where(lane < MLA_ROPE, _rope128(misc, cos, sin, lane), 0.0)
    for h in range(MLA_HEADS):
        q_lat = _nn(q_nope[:, h * MLA_NOPE:(h + 1) * MLA_NOPE].astype(BF16), wuk_ref[h]) * MLA_SCALE
        qcat_ref[:, h * 256:h * 256 + LANES] = q_lat.astype(BF16)
        qr = q_rope if h == 0 else pltpu.roll(q_rope, LANES - MLA_ROPE * h, 1)
        qcat_ref[:, h * 256 + LANES:(h + 1) * 256] = jnp.where(lane < MLA_ROPE, qr, 0.0).astype(BF16)
    kcat_ref[:, :LANES] = lat.astype(BF16)
    kcat_ref[:, LANES:] = kr.astype(BF16)
    lat_ref[...] = lat
    kr_ref[...] = kr
    if with_nsa:
        kvb_ref[...] = zn_ref[:, Z_KS - Z_KC:Z_VW - Z_KC + LANES].astype(BF16)
        nb = tb // NSA_L_CMP
        kcb = zn_ref[:, 0:LANES].reshape(nb, NSA_L_CMP, LANES) * gk_ref[...][None]
        vcb = zn_ref[:, LANES:2 * LANES].reshape(nb, NSA_L_CMP, LANES) * gv_ref[...][None]
        kc_ref[...] = jnp.sum(kcb, axis=1) * (1.0 / NSA_L_CMP)
        vc_ref[...] = jnp.sum(vcb, axis=1) * (1.0 / NSA_L_CMP)


def _prep(z, cos, sin, lw, *, tb, rows_per_table, with_nsa):
    m = z.shape[0]
    nt = rows_per_table // tb
    row = lambda i: (i, 0)
    full2 = lambda i: (0, 0)
    in_specs = [pl.BlockSpec((tb, 512), lambda i: (i, Z_CQ // 512)),
                pl.BlockSpec((tb, 1024), lambda i: (i, Z_KC // 1024)),
                pl.BlockSpec((tb, LANES), lambda i: (i % nt, 0)),
                pl.BlockSpec((tb, LANES), lambda i: (i % nt, 0)),
                pl.BlockSpec((1, MLA_Q_LORA), full2),
                pl.BlockSpec((1, MLA_KV_LORA), full2),
                pl.BlockSpec((MLA_Q_LORA, MLA_HEADS * MLA_NOPE), full2),
                pl.BlockSpec((MLA_Q_LORA, MLA_HEADS * MLA_ROPE), full2),
                pl.BlockSpec((MLA_HEADS, MLA_NOPE, MLA_KV_LORA), lambda i: (0, 0, 0)),
                pl.BlockSpec((NSA_L_CMP, NSA_DK), full2),
                pl.BlockSpec((NSA_L_CMP, NSA_DK), full2)]
    out_specs = [pl.BlockSpec((tb, 1024), row), pl.BlockSpec((tb, 256), row),
                 pl.BlockSpec((tb, LANES), row), pl.BlockSpec((tb, LANES), row)]
    out_shape = [jax.ShapeDtypeStruct((m, 1024), BF16), jax.ShapeDtypeStruct((m, 256), BF16),
                 jax.ShapeDtypeStruct((m, LANES), F32), jax.ShapeDtypeStruct((m, LANES), F32)]
    if with_nsa:
        nb = tb // NSA_L_CMP
        out_specs += [pl.BlockSpec((tb, 512), row), pl.BlockSpec((nb, LANES), row), pl.BlockSpec((nb, LANES), row)]
        out_shape += [jax.ShapeDtypeStruct((m, 512), BF16),
                      jax.ShapeDtypeStruct((m // NSA_L_CMP, LANES), F32),
                      jax.ShapeDtypeStruct((m // NSA_L_CMP, LANES), F32)]
    return pl.pallas_call(
        functools.partial(_prep_kernel, tb=tb, with_nsa=with_nsa),
        grid=(m // tb,),
        in_specs=in_specs, out_specs=out_specs, out_shape=out_shape,
        compiler_params=_params(1),
        name="prep_nsa" if with_nsa else "prep",
    )(z, z, cos, sin, lw["q_norm"], lw["kv_norm"], lw["w_uq_nope"], lw["w_uq_rope"], lw["w_ukT"],
      lw["gain_k"], lw["gain_v"])


def _online_step(s, v, m_ref, l_ref, acc_ref):
    s0, s1 = s[:, :LANES], s[:, LANES:]
    m_prev = m_ref[...]
    m_new = jnp.maximum(m_prev, jnp.max(jnp.maximum(s0, s1), axis=1, keepdims=True))
    alpha = jnp.exp(m_prev - m_new)
    p0 = jnp.exp(s0 - m_new)
    p1 = jnp.exp(s1 - m_new)
    l_ref[...] = alpha * l_ref[...] + jnp.sum(p0 + p1, axis=1, keepdims=True)
    p = jnp.concatenate([p0, p1], axis=1).astype(BF16)
    acc_ref[...] = alpha * acc_ref[...] + _nn(p, v)
    m_ref[...] = m_new


def _reset_state(m_ref, l_ref, acc_ref):
    m_ref[...] = jnp.full(m_ref.shape, M_INIT, F32)
    l_ref[...] = jnp.zeros(l_ref.shape, F32)
    acc_ref[...] = jnp.zeros(acc_ref.shape, F32)


def _stack_heads(x):
    return jnp.concatenate([x] * NSA_HEADS, axis=0)


def _mla_prompt_kernel(q_ref, k_ref, o_ref, m_ref, l_ref, acc_ref):
    qi = pl.program_id(1)
    _reset_state(m_ref, l_ref, acc_ref)
    q_all = jnp.concatenate([q_ref[:, h * 256:(h + 1) * 256] for h in range(MLA_HEADS)], axis=0)
    rc = (lax.broadcasted_iota(jnp.int32, (Q_BLOCK, K_BLOCK), 0)
          - lax.broadcasted_iota(jnp.int32, (Q_BLOCK, K_BLOCK), 1))

    def body(kb, carry):
        k0 = pl.multiple_of(kb * K_BLOCK, K_BLOCK)
        k = k_ref[pl.ds(k0, K_BLOCK), :]
        pen = jnp.where(rc + (qi * Q_BLOCK - kb * K_BLOCK) >= 0, 0.0, NEG_INF)
        _online_step(_nt(q_all, k) + _stack_heads(pen), k[:, :MLA_KV_LORA], m_ref, l_ref, acc_ref)
        return carry

    lax.fori_loop(0, (qi * Q_BLOCK) // K_BLOCK + 1, body, 0)
    out = acc_ref[...] / jnp.maximum(l_ref[...], 1e-30)
    for h in range(MLA_HEADS):
        o_ref[:, h * LANES:(h + 1) * LANES] = out[h * Q_BLOCK:(h + 1) * Q_BLOCK]


def _mla_prompt(qcat, kcat, *, batch, seq):
    nq = seq // Q_BLOCK
    return pl.pallas_call(
        _mla_prompt_kernel,
        grid=(batch, nq),
        in_specs=[pl.BlockSpec((Q_BLOCK, 1024), lambda b, i: (b * nq + i, 0)),
                  pl.BlockSpec((seq, 256), lambda b, i: (b, 0))],
        out_specs=pl.BlockSpec((Q_BLOCK, 512), lambda b, i: (b * nq + i, 0)),
        out_shape=jax.ShapeDtypeStruct((batch * seq, 512), F32),
        scratch_shapes=[pltpu.VMEM((MLA_HEADS * Q_BLOCK, LANES), F32)] * 3,
        compiler_params=_params(2),
        name="mla_prompt",
    )(qcat, kcat)


def _nsa_prompt_kernel(q_ref, misc_ref, kc_ref, vc_ref, kvb_ref, e_ref, o_ref,
                       selmask_ref, m_ref, l_ref, acc_ref, *, seq, n_sel):
    qi = pl.program_id(1)
    t0 = qi * Q_BLOCK
    nc = seq // NSA_L_CMP
    q = q_ref[...] * NSA_SCALE
    qh = [q[:, h * NSA_DK:(h + 1) * NSA_DK].astype(BF16) for h in range(NSA_HEADS)]
    gates = jax.nn.sigmoid(misc_ref[...])

    def gate(j, h):
        c = MISC_GATE0 + NSA_HEADS * j + h
        return gates[:, c:c + 1]

    kc = kc_ref[...].astype(BF16)
    vc = vc_ref[...].astype(BF16)
    t_a = t0 + lax.broadcasted_iota(jnp.int32, (Q_BLOCK, nc), 0)
    bend_a = lax.broadcasted_iota(jnp.int32, (Q_BLOCK, nc), 1) * NSA_L_CMP + (NSA_L_CMP - 1)
    mask_a = bend_a <= t_a
    dist_a = (t_a - bend_a).astype(F32)
    t_b = t0 + lax.broadcasted_iota(jnp.int32, (nc, Q_BLOCK), 1)
    bend_b = lax.broadcasted_iota(jnp.int32, (nc, Q_BLOCK), 0) * NSA_L_CMP + (NSA_L_CMP - 1)
    mask_b = bend_b <= t_b
    dist_b = (t_b - bend_b).astype(F32)
    imp = jnp.zeros((nc, Q_BLOCK), F32)
    for h in range(NSA_HEADS):
        s_a = jnp.where(mask_a, _nt(qh[h], kc) - ALIBI[h] * dist_a, NEG_INF)
        e_a = jnp.where(mask_a, jnp.exp(s_a - jnp.max(s_a, axis=1, keepdims=True)), 0.0)
        p_a = e_a / jnp.maximum(jnp.sum(e_a, axis=1, keepdims=True), 1e-30)
        o_ref[:, h * NSA_DK:(h + 1) * NSA_DK] = gate(0, h) * _nn(p_a.astype(BF16), vc)
        s_b = jnp.where(mask_b, _nt(kc, qh[h]) - ALIBI[h] * dist_b, NEG_INF)
        e_b = jnp.where(mask_b, jnp.exp(s_b - jnp.max(s_b, axis=0, keepdims=True)), 0.0)
        imp = imp + e_b / jnp.maximum(jnp.sum(e_b, axis=0, keepdims=True), 1e-30)

    imp2 = imp + pltpu.roll(imp, nc - 1, 0)
    ridx = lax.broadcasted_iota(jnp.int32, (nc, Q_BLOCK), 0)
    blk = ridx >> 1
    cur = t_b >> 6
    forced = (blk == 0) | (blk == cur) | (blk == cur - 1)
    score = jnp.where(blk <= cur, jnp.where(forced, FORCED_SCORE, imp2), -FORCED_SCORE)
    rank = jnp.zeros((nc, Q_BLOCK), F32)
    for i in range(0, nc, 2):
        s_i = score[i:i + 1, :]
        ahead = (s_i > score) | ((s_i == score) & (ridx > i))
        rank = rank + jnp.where(ahead, 1.0, 0.0)
    sel_t = jnp.where((rank < n_sel) & ((ridx & 1) == 0), 1.0, 0.0).astype(BF16)
    selmask_ref[...] = (_tn(sel_t, e_ref[...]) - 1.0) * (-NEG_INF)

    q_all = jnp.concatenate(qh, axis=0)
    rc = (lax.broadcasted_iota(jnp.int32, (Q_BLOCK, K_BLOCK), 0)
          - lax.broadcasted_iota(jnp.int32, (Q_BLOCK, K_BLOCK), 1))

    def branch(first_kb, col_k, col_v, j, use_sel):
        _reset_state(m_ref, l_ref, acc_ref)

        def body(kb, carry):
            k0 = pl.multiple_of(kb * K_BLOCK, K_BLOCK)
            k = kvb_ref[pl.ds(k0, K_BLOCK), col_k:col_k + NSA_DK]
            v = kvb_ref[pl.ds(k0, K_BLOCK), col_v:col_v + NSA_DK]
            dist_i = rc + (qi * Q_BLOCK - kb * K_BLOCK)
            dist = dist_i.astype(F32)
            if use_sel:
                base = jnp.where(dist_i >= 0, 0.0, NEG_INF) + selmask_ref[:, pl.ds(k0, K_BLOCK)]
            else:
                base = jnp.where(dist_i >= 0, jnp.where(dist_i <= NSA_WINDOW, 0.0, NEG_INF), NEG_INF)
            pen = jnp.concatenate([base - ALIBI[h] * dist for h in range(NSA_HEADS)], axis=0)
            _online_step(_nt(q_all, k) + pen, v, m_ref, l_ref, acc_ref)
            return carry

        lax.fori_loop(first_kb, (qi * Q_BLOCK) // K_BLOCK + 1, body, 0)
        out = acc_ref[...] / jnp.maximum(l_ref[...], 1e-30)
        for h in range(NSA_HEADS):
            o_ref[:, h * NSA_DK:(h + 1) * NSA_DK] += gate(j, h) * out[h * Q_BLOCK:(h + 1) * Q_BLOCK]

    branch(0, 0, NSA_DK, 1, True)
    branch(jnp.maximum(qi * Q_BLOCK - NSA_WINDOW, 0) // K_BLOCK, 2 * NSA_DK, 3 * NSA_DK, 2, False)


def _nsa_prompt(z, kc, vc, kvb, *, batch, seq):
    nq = seq // Q_BLOCK
    nc = seq // NSA_L_CMP
    n_sb = -(-seq // NSA_L_SEL)
    expand = np.zeros((nc, seq), np.float32)
    for j in range(n_sb):
        expand[2 * j, j * NSA_L_SEL:(j + 1) * NSA_L_SEL] = 1.0
    return pl.pallas_call(
        functools.partial(_nsa_prompt_kernel, seq=seq, n_sel=min(NSA_TOPN, n_sb)),
        grid=(batch, nq),
        in_specs=[pl.BlockSpec((Q_BLOCK, 512), lambda b, i: (b * nq + i, Z_NSA_Q // 512)),
                  pl.BlockSpec((Q_BLOCK, LANES), lambda b, i: (b * nq + i, Z_MISC // LANES)),
                  pl.BlockSpec((nc, LANES), lambda b, i: (b, 0)),
                  pl.BlockSpec((nc, LANES), lambda b, i: (b, 0)),
                  pl.BlockSpec((seq, 512), lambda b, i: (b, 0)),
                  pl.BlockSpec((nc, seq), lambda b, i: (0, 0))],
        out_specs=pl.BlockSpec((Q_BLOCK, 512), lambda b, i: (b * nq + i, 0)),
        out_shape=jax.ShapeDtypeStruct((batch * seq, 512), F32),
        scratch_shapes=[pltpu.VMEM((Q_BLOCK, seq), F32)] + [pltpu.VMEM((NSA_HEADS * Q_BLOCK, LANES), F32)] * 3,
        compiler_params=_params(2),
        name="nsa_prompt",
    )(z, z, kc, vc, kvb, jnp.asarray(expand, BF16))


def _silu(x):
    return x * jax.nn.sigmoid(x)


def _mix_and_project(x, y_pool, olat, y_conv, y_nsa, g_pool, g_mla, g_conv, g_nsa, wuv_ref, wout_ref):
    y_mla = jnp.concatenate(
        [_nn(olat[:, h * MLA_KV_LORA:(h + 1) * MLA_KV_LORA].astype(BF16), wuv_ref[h]) for h in range(MLA_HEADS)],
        axis=1)
    mix = jnp.concatenate([y_pool * _silu(g_pool), y_mla * _silu(g_mla),
                           y_conv * _silu(g_conv), y_nsa * _silu(g_nsa)], axis=1)
    return x + _nn(mix.astype(BF16), wout_ref[...])


def _pool_project(d, pw_ref, ps_ref):
    return jnp.concatenate(
        [_nn(d[:, g * POOL_CH:(g + 1) * POOL_CH].astype(BF16), pw_ref[g]) for g in range(len(POOL_WINDOWS))],
        axis=1) * ps_ref[...]


def _out_prompt_kernel(x_ref, u_ref, uh_ref, gp_ref, gm_ref, ch_ref, cc_ref, cb_ref, gc_ref, gn_ref,
                       chh_ref, cch_ref, olat_ref, ynsa_ref, pw_ref, ps_ref, wuv_ref, cw_ref, cbias_ref,
                       wout_ref, fg_ref, *out_refs, tb, seq, final):
    if final:
        xo_ref, y_ref, zc_ref = out_refs
    else:
        xo_ref, zc_ref = out_refs
    i = pl.program_id(0)
    t0 = (i * tb) % seq
    first = t0 == 0
    u = u_ref[...]
    full = jnp.concatenate([jnp.where(first, 0.0, uh_ref[...]), u], axis=0)
    t = (t0 + lax.broadcasted_iota(jnp.int32, (tb, POOL_CH), 0) + 1).astype(F32)
    sums = full
    means = []
    step = 1
    for g, w in enumerate(POOL_WINDOWS):
        while step < w:
            sums = sums + pltpu.roll(sums, step, 0)
            step *= 2
        sg = sums[:, g * POOL_CH:(g + 1) * POOL_CH]
        means.append(sg[POOL_HIST + 1:, :] / jnp.minimum(float(w), t))
    d = jnp.concatenate(means, axis=1) - u
    y_pool = _pool_project(d, pw_ref, ps_ref)
    zc = cc_ref[...] * ch_ref[...]
    zfull = jnp.concatenate([jnp.where(first, 0.0, cch_ref[...] * chh_ref[...]), zc], axis=0)
    cw = cw_ref[...]
    conv = (cw[0:1] * pltpu.roll(zfull, 2, 0)[8:] + cw[1:2] * pltpu.roll(zfull, 1, 0)[8:]
            + cw[2:3] * zc + cbias_ref[...])
    y_conv = cb_ref[...] * conv
    zc_ref[...] = zc[tb - 8:, :]
    out = _mix_and_project(x_ref[...], y_pool, olat_ref[...], y_conv, ynsa_ref[...],
                           gp_ref[...], gm_ref[...], gc_ref[...], gn_ref[...], wuv_ref, wout_ref)
    xo_ref[...] = out
    if final:
        y_ref[...] = _rms(out, fg_ref[...])


def _out_prompt(x, z, olat, ynsa, lw, final_gain, *, seq, tb, final):
    m = x.shape[0]
    zblk = lambda c: pl.BlockSpec((tb, 512), lambda i: (i, c // 512))
    halo16 = pl.BlockSpec((16, 512), lambda i: (jnp.maximum(i * (tb // 16) - 1, 0), Z_POOL_U // 512))
    halo8 = lambda c: pl.BlockSpec((8, 512), lambda i: (jnp.maximum(i * (tb // 8) - 1, 0), c // 512))
    row = lambda w: pl.BlockSpec((tb, w), lambda i: (i, 0))
    cst2 = lambda a, b: pl.BlockSpec((a, b), lambda i: (0, 0))
    cst3 = lambda a, b, c: pl.BlockSpec((a, b, c), lambda i: (0, 0, 0))
    in_specs = [row(D_MODEL), zblk(Z_POOL_U), halo16, zblk(Z_POOL_G), zblk(Z_MLA_G),
                zblk(Z_CONV_H), zblk(Z_CONV_C), zblk(Z_CONV_B), zblk(Z_CONV_G), zblk(Z_NSA_G),
                halo8(Z_CONV_H), halo8(Z_CONV_C), row(512), row(512),
                cst3(len(POOL_WINDOWS), POOL_CH, POOL_CH), cst2(1, D_GROUP),
                cst3(MLA_HEADS, MLA_KV_LORA, 128), cst2(CONV_WIDTH, D_GROUP), cst2(1, D_GROUP),
                cst2(D_MODEL, D_MODEL), cst2(1, D_MODEL)]
    out_specs = [row(D_MODEL)] + ([row(D_MODEL)] if final else []) + [pl.BlockSpec((8, 512), lambda i: (i, 0))]
    out_shape = ([jax.ShapeDtypeStruct((m, D_MODEL), F32)] * (2 if final else 1)
                 + [jax.ShapeDtypeStruct((m // tb * 8, 512), F32)])
    return pl.pallas_call(
        functools.partial(_out_prompt_kernel, tb=tb, seq=seq, final=final),
        grid=(m // tb,),
        in_specs=in_specs, out_specs=out_specs, out_shape=out_shape,
        compiler_params=_params(1),
        name="out_prompt_final" if final else "out_prompt",
    )(x, z, z, z, z, z, z, z, z, z, z, z, olat, ynsa, lw["pool_w"], lw["pool_scale"], lw["w_uv"],
      lw["conv_w"], lw["conv_b"], lw["w_out"], final_gain)


def _out_sample_kernel(x_ref, z_ref, pst_ref, c0_ref, c1_ref, olat_ref, ynsa_ref, pw_ref, ps_ref, wuv_ref,
                       cw_ref, cbias_ref, wout_ref, fg_ref, *out_refs, n_past, final):
    if final:
        xo_ref, y_ref, zc_ref = out_refs
    else:
        xo_ref, zc_ref = out_refs
    zcol = lambda c: z_ref[:, c:c + 512]
    u = zcol(Z_POOL_U)
    sums = u
    means = []
    back = 1
    for g, w in enumerate(POOL_WINDOWS):
        while back < w:
            sums = sums + pst_ref[POOL_HIST - back]
            back += 1
        means.append(sums[:, g * POOL_CH:(g + 1) * POOL_CH] / float(min(w, n_past + 1)))
    d = jnp.concatenate(means, axis=1) - u
    y_pool = _pool_project(d, pw_ref, ps_ref)
    zc = zcol(Z_CONV_C) * zcol(Z_CONV_H)
    cw = cw_ref[...]
    conv = cw[0:1] * c0_ref[...] + cw[1:2] * c1_ref[...] + cw[2:3] * zc + cbias_ref[...]
    y_conv = zcol(Z_CONV_B) * conv
    zc_ref[...] = zc
    out = _mix_and_project(x_ref[...], y_pool, olat_ref[...], y_conv, ynsa_ref[...],
                           zcol(Z_POOL_G), zcol(Z_MLA_G), zcol(Z_CONV_G), zcol(Z_NSA_G), wuv_ref, wout_ref)
    xo_ref[...] = out
    if final:
        y_ref[...] = _rms(out, fg_ref[...])


def _out_sample(x, z, pool_state_t, conv0, conv1, olat, ynsa, lw, final_gain, *, n_past, final):
    m = x.shape[0]
    n_out = 3 if final else 2
    out_shape = ([jax.ShapeDtypeStruct((m, D_MODEL), F32)] * (n_out - 1) + [jax.ShapeDtypeStruct((m, 512), F32)])
    return pl.pallas_call(
        functools.partial(_out_sample_kernel, n_past=n_past, final=final),
        out_shape=out_shape,
        compiler_params=pltpu.CompilerParams(vmem_limit_bytes=VMEM_LIMIT),
        name="out_sample_final" if final else "out_sample",
    )(x, z, pool_state_t, conv0, conv1, olat, ynsa, lw["pool_w"], lw["pool_scale"], lw["w_uv"],
      lw["conv_w"], lw["conv_b"], lw["w_out"], final_gain)


HEAD_ROWS = 16
PAGE_SLOTS = 3


def _rows_to_row(o):
    return jnp.concatenate([o[h:h + 1, :] for h in range(NSA_HEADS)], axis=1)


def _sample_paged_kernel(pt_ref, lat_hbm, kr_hbm, ck_hbm, cv_hbm, qm_ref, qr_ref, qn_ref, latn_ref, krn_ref,
                         gk_ref, gv_ref, olat_ref, ocmp_ref, imp_ref,
                         lat_buf, kr_buf, ck_buf, cv_buf, sem, m_ref, l_ref, acc_ref, kc_ref, vc_ref,
                         *, layer, cp, n_chunks, n_past, batch):
    b = pl.program_id(0)
    krows = MLA_ROPE

    def copies(row, chunk, slot):
        out = []
        for i in range(cp):
            page = pt_ref[row, chunk * cp + i]
            out.append(pltpu.make_async_copy(lat_hbm.at[layer, page], lat_buf.at[slot, pl.ds(i * PAGE_SIZE, PAGE_SIZE)],
                                             sem.at[slot]))
            out.append(pltpu.make_async_copy(kr_hbm.at[layer, page], kr_buf.at[slot, pl.ds(i * krows, krows)],
                                             sem.at[slot]))
            out.append(pltpu.make_async_copy(ck_hbm.at[layer, page], ck_buf.at[slot, pl.ds(i * PAGE_SIZE, PAGE_SIZE)],
                                             sem.at[slot]))
            out.append(pltpu.make_async_copy(cv_hbm.at[layer, page], cv_buf.at[slot, pl.ds(i * PAGE_SIZE, PAGE_SIZE)],
                                             sem.at[slot]))
        return out

    total = batch * n_chunks

    def chunk_copies(g):
        gw = lax.rem(g, total)
        return copies(gw // n_chunks, lax.rem(gw, n_chunks), lax.rem(g, PAGE_SLOTS))

    @pl.when(b == 0)
    def _():
        for g0 in range(PAGE_SLOTS - 1):
            for cpy in chunk_copies(g0):
                cpy.start()

    m_ref[...] = jnp.full(m_ref.shape, M_INIT, F32)
    l_ref[...] = jnp.zeros(l_ref.shape, F32)
    acc_ref[...] = jnp.zeros(acc_ref.shape, F32)
    qm = qm_ref[...]
    qr = qr_ref[:, :MLA_ROPE]
    gk = gk_ref[...][None]
    gv = gv_ref[...][None]
    bpc = cp * PAGE_SIZE // NSA_L_CMP

    def chunk_body(c, carry):
        g = b * n_chunks + c
        slot = lax.rem(g, PAGE_SLOTS)
        for cpy in copies(b, c, slot):
            cpy.wait()
        for cpy in chunk_copies(g + PAGE_SLOTS - 1):
            cpy.start()

        lat = lat_buf[slot].astype(BF16)
        kr = kr_buf[slot].astype(BF16)
        s_rope = jnp.concatenate([_nn(qr, kr[i * krows:(i + 1) * krows, :]) for i in range(cp)], axis=1)
        s = _nt(qm, lat) + s_rope
        m_prev = m_ref[...]
        m_new = jnp.maximum(m_prev, jnp.max(s, axis=1, keepdims=True))
        alpha = jnp.exp(m_prev - m_new)
        p = jnp.exp(s - m_new[:, 0:1])
        l_ref[...] = alpha * l_ref[...] + jnp.sum(p, axis=1, keepdims=True)
        acc_ref[...] = alpha * acc_ref[...] + _nn(p.astype(BF16), lat)
        m_ref[...] = m_new
        r0 = pl.multiple_of(c * bpc, bpc)
        kc_ref[pl.ds(r0, bpc), :] = jnp.sum(ck_buf[slot].reshape(bpc, NSA_L_CMP, NSA_DK) * gk, axis=1) * (1.0 / NSA_L_CMP)
        vc_ref[pl.ds(r0, bpc), :] = jnp.sum(cv_buf[slot].reshape(bpc, NSA_L_CMP, NSA_DK) * gv, axis=1) * (1.0 / NSA_L_CMP)
        return carry

    lax.fori_loop(0, n_chunks, chunk_body, 0)

    @pl.when(b == batch - 1)
    def _():
        for g0 in range(PAGE_SLOTS - 1):
            for cpy in chunk_copies(total + g0):
                cpy.wait()

    latn = latn_ref[...].astype(BF16)
    is_new = lax.broadcasted_iota(jnp.int32, (HEAD_ROWS, HEAD_ROWS), 1) == 0
    s_n = jnp.where(is_new, _nt(qm, latn) + _nt(qr, krn_ref[:, :MLA_ROPE].astype(BF16)), NEG_INF)
    m_prev = m_ref[...]
    m_f = jnp.maximum(m_prev, jnp.max(s_n, axis=1, keepdims=True))
    a_f = jnp.exp(m_prev - m_f)
    p_n = jnp.where(is_new, jnp.exp(s_n - m_f[:, 0:HEAD_ROWS]), 0.0)
    l_f = l_ref[...] * a_f + jnp.sum(p_n, axis=1, keepdims=True)
    acc_f = acc_ref[...] * a_f + _nn(p_n.astype(BF16), latn)
    olat_ref[...] = _rows_to_row(acc_f / jnp.maximum(l_f, 1e-30))

    nblk = n_past // NSA_L_CMP
    s = _nt(qn_ref[...], kc_ref[...].astype(BF16))
    hrow = lax.broadcasted_iota(jnp.int32, (HEAD_ROWS, nblk), 0)
    bend = lax.broadcasted_iota(jnp.int32, (HEAD_ROWS, nblk), 1) * NSA_L_CMP + (NSA_L_CMP - 1)
    slope = functools.reduce(lambda a, h: jnp.where(hrow == h, ALIBI[h], a), range(NSA_HEADS),
                             jnp.zeros((HEAD_ROWS, nblk), F32))
    valid = bend <= n_past
    s = jnp.where(valid, s - slope * (n_past - bend).astype(F32), NEG_INF)
    e = jnp.where(valid, jnp.exp(s - jnp.max(s, axis=1, keepdims=True)), 0.0)
    p = e / jnp.maximum(jnp.sum(e, axis=1, keepdims=True), 1e-30)
    imp_ref[...] = jnp.sum(jnp.where(hrow < NSA_HEADS, p, 0.0), axis=0, keepdims=True)
    ocmp_ref[...] = _rows_to_row(_nn(p.astype(BF16), vc_ref[...].astype(BF16)))


def _sample_paged(page_table, c_lat, c_kr_t, c_ck, c_cv, qm, qr, qn, latn, krn, gk, gv, *, layer, cp):
    batch, n_pages = page_table.shape
    n_chunks = n_pages // cp
    n_past = n_pages * PAGE_SIZE
    nblk = n_past // NSA_L_CMP
    hbm = pl.BlockSpec(memory_space=pl.ANY)
    per_b3 = lambda r, w: pl.BlockSpec((None, r, w), lambda b, pt: (b, 0, 0))
    cst2 = lambda a, w: pl.BlockSpec((a, w), lambda b, pt: (0, 0))
    in_specs = [hbm, hbm, hbm, hbm,
                per_b3(HEAD_ROWS, LANES), per_b3(HEAD_ROWS, LANES), per_b3(HEAD_ROWS, LANES),
                per_b3(HEAD_ROWS, LANES), per_b3(HEAD_ROWS, LANES), cst2(NSA_L_CMP, NSA_DK), cst2(NSA_L_CMP, NSA_DK)]
    out_specs = [per_b3(1, 512), per_b3(1, 512), per_b3(1, nblk)]
    out_shape = [jax.ShapeDtypeStruct((batch, 1, 512), F32), jax.ShapeDtypeStruct((batch, 1, 512), F32),
                 jax.ShapeDtypeStruct((batch, 1, nblk), F32)]
    grid_spec = pltpu.PrefetchScalarGridSpec(
        num_scalar_prefetch=1, grid=(batch,), in_specs=in_specs, out_specs=out_specs,
        scratch_shapes=[pltpu.VMEM((PAGE_SLOTS, cp * PAGE_SIZE, MLA_KV_LORA), F32),
                        pltpu.VMEM((PAGE_SLOTS, cp * MLA_ROPE, PAGE_SIZE), F32),
                        pltpu.VMEM((PAGE_SLOTS, cp * PAGE_SIZE, NSA_DK), F32),
                        pltpu.VMEM((PAGE_SLOTS, cp * PAGE_SIZE, NSA_DK), F32),
                        pltpu.SemaphoreType.DMA((PAGE_SLOTS,)),
                        pltpu.VMEM((HEAD_ROWS, LANES), F32), pltpu.VMEM((HEAD_ROWS, LANES), F32),
                        pltpu.VMEM((HEAD_ROWS, MLA_KV_LORA), F32),
                        pltpu.VMEM((nblk, NSA_DK), F32), pltpu.VMEM((nblk, NSA_DK), F32)])
    return pl.pallas_call(
        functools.partial(_sample_paged_kernel, layer=layer, cp=cp, n_chunks=n_chunks, n_past=n_past, batch=batch),
        grid_spec=grid_spec, out_shape=out_shape,
        compiler_params=_params(1),
        name="sample_paged",
    )(page_table, c_lat, c_kr_t, c_ck, c_cv, qm, qr, qn, latn, krn, gk, gv)


def _select_kernel(imp_ref, idx_ref, *, n_past, n_sel):
    rows, nblk = imp_ref.shape
    ratio = NSA_L_SEL // NSA_L_CMP
    n_sb = -(-(n_past + 1) // NSA_L_SEL)
    n_in = nblk // ratio
    cur = n_past // NSA_L_SEL
    x = imp_ref[...]
    pair = x + pltpu.roll(x, nblk - 1, 1)
    lane = lax.broadcasted_iota(jnp.int32, (rows, nblk + LANES), 1)
    blk = jnp.where(lane < nblk, lane >> 1, n_in + lane - nblk)
    exists = ((lane < nblk) & ((lane & 1) == 0)) | ((lane >= nblk) & (blk < n_sb))
    imp = jnp.concatenate([pair, jnp.zeros((rows, LANES), F32)], axis=1)
    forced = (blk == 0) | (blk == cur) | (blk == cur - 1)
    score = jnp.where(blk * NSA_L_SEL <= n_past, jnp.where(forced, FORCED_SCORE, imp), -FORCED_SCORE)
    score = jnp.where(exists, score, -3e38)
    lane_f = lane.astype(F32)
    out_lane = lax.broadcasted_iota(jnp.int32, (rows, LANES), 1)
    out = jnp.zeros((rows, LANES), jnp.int32)
    for r in range(n_sel):
        top = jnp.max(score, axis=1, keepdims=True)
        first = jnp.min(jnp.where(score == top, lane_f, 1e9), axis=1, keepdims=True)
        first_i = first.astype(jnp.int32)
        chosen = jnp.where(first_i < nblk, first_i >> 1, n_in + first_i - nblk)
        out = jnp.where(out_lane == r, chosen, out)
        score = jnp.where(lane_f == first, -3e38, score)
    idx_ref[...] = out


def _select(imp, *, n_past, n_sel):
    rows = imp.shape[0]
    return pl.pallas_call(
        functools.partial(_select_kernel, n_past=n_past, n_sel=n_sel),
        out_shape=jax.ShapeDtypeStruct((rows, LANES), jnp.int32),
        compiler_params=pltpu.CompilerParams(vmem_limit_bytes=VMEM_LIMIT),
        name="select",
    )(imp)


def _sample_nsa_kernel(ph_ref, hf_ref, blk_ref, sk_hbm, sv_hbm, qn_ref, ksn_ref, vsn_ref, wk_ref, wv_ref,
                       kwn_ref, vwn_ref, misc_ref, ocmp_ref, o_ref, wko_ref, wvo_ref, k_buf, v_buf, sem,
                       *, layer, n_sel, n_past, wb, batch):
    b = pl.program_id(0)

    def copies(row, slot):
        out = []
        for i in range(n_sel):
            src = pl.ds(pl.multiple_of(hf_ref[row, i] * NSA_L_SEL, NSA_L_SEL), NSA_L_SEL)
            dst = pl.ds(i * NSA_L_SEL, NSA_L_SEL)
            out.append(pltpu.make_async_copy(sk_hbm.at[layer, ph_ref[row, i], src], k_buf.at[slot, dst], sem.at[slot]))
            out.append(pltpu.make_async_copy(sv_hbm.at[layer, ph_ref[row, i], src], v_buf.at[slot, dst], sem.at[slot]))
        return out

    @pl.when(b == 0)
    def _():
        for cpy in copies(0, 0):
            cpy.start()

    slot = b % 2

    @pl.when(b + 1 < batch)
    def _():
        for cpy in copies(b + 1, 1 - slot):
            cpy.start()

    for cpy in copies(b, slot):
        cpy.wait()
    k_blocks = [k_buf[slot, i * NSA_L_SEL:(i + 1) * NSA_L_SEL, :] for i in range(n_sel)]
    v_blocks = [v_buf[slot, i * NSA_L_SEL:(i + 1) * NSA_L_SEL, :] for i in range(n_sel)]
    qn = qn_ref[...]
    hcol = lax.broadcasted_iota(jnp.int32, (HEAD_ROWS, 1), 0)
    slope = functools.reduce(lambda a, h: jnp.where(hcol == h, ALIBI[h], a), range(NSA_HEADS),
                             jnp.zeros((HEAD_ROWS, 1), F32))
    is_new = lax.broadcasted_iota(jnp.int32, (HEAD_ROWS, HEAD_ROWS), 1) == 0

    def attend(keys, values, dists, k_new, v_new):
        scores = []
        for k, d in zip(keys, dists):
            scores.append(jnp.where(d >= 0, _nt(qn, k.astype(BF16)) - slope * d.astype(F32), NEG_INF))
        k_new = k_new.astype(BF16)
        s_n = jnp.where(is_new, _nt(qn, k_new), NEG_INF)
        m = functools.reduce(jnp.maximum, [jnp.max(s, axis=1, keepdims=True) for s in scores + [s_n]])
        p_n = jnp.where(is_new, jnp.exp(s_n - m), 0.0)
        lsum = jnp.sum(p_n, axis=1, keepdims=True)
        acc = _nn(p_n.astype(BF16), v_new.astype(BF16))
        for s, v, d in zip(scores, values, dists):
            p = jnp.where(d >= 0, jnp.exp(s - m), 0.0)
            lsum = lsum + jnp.sum(p, axis=1, keepdims=True)
            acc = acc + _nn(p.astype(BF16), v.astype(BF16))
        return _rows_to_row(acc / jnp.maximum(lsum, 1e-30))

    off = lax.broadcasted_iota(jnp.int32, (1, NSA_L_SEL), 1)
    dists = []
    for i in range(n_sel):
        blk = blk_ref[b, i]
        dists.append(jnp.where(blk < n_past // NSA_L_SEL, n_past - blk * NSA_L_SEL, -1) - off)
    o_sel = attend(k_blocks, v_blocks, dists, ksn_ref[...], vsn_ref[...])
    wd = wb - lax.broadcasted_iota(jnp.int32, (1, wb), 1)
    wd = jnp.where(wd <= NSA_WINDOW, wd, -1)
    o_win = attend([wk_ref[...]], [wv_ref[...]], [wd], kwn_ref[...], vwn_ref[...])
    is_last = lax.broadcasted_iota(jnp.int32, (wb, NSA_DK), 0) == wb - 1
    wko_ref[...] = jnp.where(is_last, kwn_ref[0:1, :], pltpu.roll(wk_ref[...], wb - 1, 0))
    wvo_ref[...] = jnp.where(is_last, vwn_ref[0:1, :], pltpu.roll(wv_ref[...], wb - 1, 0))
    gates = jax.nn.sigmoid(misc_ref[...])
    o_cmp = ocmp_ref[...]
    pieces = []
    for h in range(NSA_HEADS):
        sl = slice(h * NSA_DK, (h + 1) * NSA_DK)
        g = [gates[:, MISC_GATE0 + NSA_HEADS * j + h:MISC_GATE0 + NSA_HEADS * j + h + 1] for j in range(3)]
        pieces.append(g[0] * o_cmp[:, sl] + g[1] * o_sel[:, sl] + g[2] * o_win[:, sl])
    o_ref[...] = jnp.concatenate(pieces, axis=1)


def _sample_nsa(phys, half, blk, c_sk, c_sv, qn, ksn, vsn, win_k, win_v, kwn, vwn, misc, ocmp, *,
                layer, n_past):
    batch, n_sel = blk.shape
    wb = win_k.shape[2]

    hbm = pl.BlockSpec(memory_space=pl.ANY)
    per_b3 = lambda r, w: pl.BlockSpec((None, r, w), lambda b, ph, hf, bl: (b, 0, 0))
    win = pl.BlockSpec((None, None, wb, NSA_DK), lambda b, ph, hf, bl: (layer, b, 0, 0))
    in_specs = [hbm, hbm, per_b3(HEAD_ROWS, LANES), per_b3(HEAD_ROWS, LANES), per_b3(HEAD_ROWS, LANES), win, win,
                per_b3(HEAD_ROWS, LANES), per_b3(HEAD_ROWS, LANES), per_b3(1, LANES), per_b3(1, 512)]
    grid_spec = pltpu.PrefetchScalarGridSpec(
        num_scalar_prefetch=3, grid=(batch,), in_specs=in_specs,
        out_specs=[per_b3(1, 512), per_b3(wb, NSA_DK), per_b3(wb, NSA_DK)],
        scratch_shapes=[pltpu.VMEM((2, n_sel * NSA_L_SEL, NSA_DK), F32),
                        pltpu.VMEM((2, n_sel * NSA_L_SEL, NSA_DK), F32),
                        pltpu.SemaphoreType.DMA((2,))])
    return pl.pallas_call(
        functools.partial(_sample_nsa_kernel, layer=layer, n_sel=n_sel, n_past=n_past, wb=wb, batch=batch),
        grid_spec=grid_spec,
        out_shape=[jax.ShapeDtypeStruct((batch, 1, 512), F32), jax.ShapeDtypeStruct((batch, wb, NSA_DK), F32),
                   jax.ShapeDtypeStruct((batch, wb, NSA_DK), F32)],
        compiler_params=_params(1),
        name="sample_nsa",
    )(phys, half, blk, c_sk, c_sv, qn, ksn, vsn, win_k, win_v, kwn, vwn, misc, ocmp)


def _rope_tables(pos):
    half = MLA_ROPE // 2
    inv = ROPE_THETA ** (-jnp.arange(half, dtype=F32) / half)
    ang = pos.astype(F32)[:, None] * inv
    cos, sin = jnp.cos(ang), jnp.sin(ang)
    reps = LANES // MLA_ROPE
    return (jnp.tile(jnp.concatenate([cos, cos], axis=1), (1, reps)),
            jnp.tile(jnp.concatenate([-sin, sin], axis=1), (1, reps)))


def _layer_weights(l, norm_gain, w_in, w_out, pool_w, pool_scale, mla_q_norm, mla_kv_norm, mla_w_uq, mla_w_uk,
                   mla_w_uv, conv_w, conv_b, nsa_cmp_pos_k, nsa_cmp_pos_v):
    w_p = _win_layout(w_in, l, tr=256)
    uq = mla_w_uq[l].reshape(MLA_Q_LORA, MLA_HEADS, MLA_NOPE + MLA_ROPE)
    return {
        "norm": norm_gain[l][None], "w_in": w_p, "w_out": w_out[l].astype(BF16),
        "pool_w": pool_w[l].astype(BF16), "pool_scale": pool_scale[l][None],
        "q_norm": mla_q_norm[l][None], "kv_norm": mla_kv_norm[l][None],
        "w_uq_nope": uq[:, :, :MLA_NOPE].reshape(MLA_Q_LORA, -1).astype(BF16),
        "w_uq_rope": uq[:, :, MLA_NOPE:].reshape(MLA_Q_LORA, -1).astype(BF16),
        "w_ukT": jnp.transpose(mla_w_uk[l], (1, 2, 0)).astype(BF16),
        "w_uv": jnp.transpose(mla_w_uv[l], (1, 0, 2)).astype(BF16),
        "conv_w": conv_w[l], "conv_b": conv_b[l][None],
        "gain_k": nsa_cmp_pos_k[l], "gain_v": nsa_cmp_pos_v[l],
    }


def _pad_rows(a, rows):
    return jnp.pad(a, ((0, 0), (0, rows - a.shape[1]), (0, 0)))


def kernel(x_prompt, x_sample, cache_mla_latent, cache_mla_krope, cache_nsa_cmp_k, cache_nsa_cmp_v,
           cache_nsa_sel_k, cache_nsa_sel_v, state_nsa_win_k, state_nsa_win_v, state_conv, state_pool,
           page_table, norm_gain, w_in, w_out, pool_w, pool_scale, mla_q_norm, mla_kv_norm,
           mla_w_uq, mla_w_uk, mla_w_uv, conv_w, conv_b, nsa_cmp_pos_k, nsa_cmp_pos_v, final_norm):
    batch, seq, _ = x_prompt.shape
    dec, dec_seq, _ = x_sample.shape
    depth = norm_gain.shape[0]
    n_pages = page_table.shape[1]
    n_past = n_pages * PAGE_SIZE
    wb = state_nsa_win_k.shape[2]
    assert dec_seq == 1 and seq % 256 == 0 and dec % 8 == 0
    n_sel = min(NSA_TOPN, -(-(n_past + dec_seq) // NSA_L_SEL))
    cp = 16 if n_pages % 32 == 0 else (8 if n_pages % 8 == 0 else n_pages)
    cache_krope_t = jnp.swapaxes(cache_mla_krope, 2, 3)

    cos_p, sin_p = _rope_tables(jnp.arange(seq, dtype=jnp.int32))
    cos_s, sin_s = _rope_tables(jnp.full((dec,), n_past, jnp.int32))
    final_gain = final_norm[None]

    xp = x_prompt.reshape(batch * seq, D_MODEL)
    xs = x_sample.reshape(dec, D_MODEL)
    new_p, new_s = [], []
    yp = ys = None
    for l in range(depth):
        final = l == depth - 1
        lw = _layer_weights(l, norm_gain, w_in, w_out, pool_w, pool_scale, mla_q_norm, mla_kv_norm, mla_w_uq,
                            mla_w_uk, mla_w_uv, conv_w, conv_b, nsa_cmp_pos_k, nsa_cmp_pos_v)
        zp = _inproj(xp, lw["norm"], lw["w_in"], tm=1024, tn=1536)
        qcat, kcat, lat, kr, kvb, kc, vc = _prep(zp, cos_p, sin_p, lw, tb=256, rows_per_table=seq, with_nsa=True)
        olat = _mla_prompt(qcat, kcat, batch=batch, seq=seq)
        ynsa = _nsa_prompt(zp, kc, vc, kvb, batch=batch, seq=seq)
        outs = _out_prompt(xp, zp, olat, ynsa, lw, final_gain, seq=seq, tb=256, final=final)
        if final:
            xp, yp, zc_tail = outs
        else:
            xp, zc_tail = outs
        z3 = zp.reshape(batch, seq, Z_WIDTH)
        wbp = min(NSA_WINDOW, seq)
        zc_last = zc_tail.reshape(batch, seq // 256, 8, 512)[:, -1]
        new_p.append((lat.reshape(batch, seq, LANES), kr.reshape(batch, seq, LANES)[:, :, :MLA_ROPE],
                      z3[:, :, Z_KC:Z_KC + NSA_DK], z3[:, :, Z_VC:Z_VC + NSA_DK],
                      z3[:, :, Z_KS:Z_KS + NSA_DK], z3[:, :, Z_VS:Z_VS + NSA_DK],
                      z3[:, seq - wbp:, Z_KW:Z_KW + NSA_DK], z3[:, seq - wbp:, Z_VW:Z_VW + NSA_DK],
                      zc_last[:, 8 - (CONV_WIDTH - 1):], z3[:, seq - POOL_HIST:, Z_POOL_U:Z_POOL_U + D_GROUP]))
        zs = _inproj(xs, lw["norm"], lw["w_in"], tm=dec, tn=1536)
        qcat_s, _, lat_s, kr_s = _prep(zs, cos_s, sin_s, lw, tb=dec, rows_per_table=dec, with_nsa=False)
        q4 = qcat_s.reshape(dec, MLA_HEADS, 256)
        qm = _pad_rows(q4[:, :, :LANES], HEAD_ROWS)
        qr = _pad_rows(q4[:, :, LANES:], HEAD_ROWS)
        qn = _pad_rows((zs[:, Z_NSA_Q:Z_NSA_Q + D_GROUP] * NSA_SCALE).astype(BF16).reshape(dec, NSA_HEADS, NSA_DK),
                       HEAD_ROWS)
        new_row = lambda a: _pad_rows(a[:, None, :], HEAD_ROWS)
        olat_s, ocmp_s, imp = _sample_paged(
            page_table, cache_mla_latent, cache_krope_t, cache_nsa_cmp_k, cache_nsa_cmp_v,
            qm, qr, qn, new_row(lat_s), new_row(kr_s), lw["gain_k"], lw["gain_v"], layer=l, cp=cp)
        blk = _select(imp.reshape(dec, -1), n_past=n_past, n_sel=n_sel)[:, :n_sel]
        page = jnp.minimum(blk // (PAGE_SIZE // NSA_L_SEL), n_pages - 1)
        phys = jnp.take_along_axis(page_table, page, axis=1)
        half = blk % (PAGE_SIZE // NSA_L_SEL)
        zcol = lambda c, w=NSA_DK: zs[:, c:c + w]
        ynsa_s, win_k_new, win_v_new = _sample_nsa(
            phys, half, blk, cache_nsa_sel_k, cache_nsa_sel_v, qn, new_row(zcol(Z_KS)), new_row(zcol(Z_VS)),
            state_nsa_win_k, state_nsa_win_v, new_row(zcol(Z_KW)), new_row(zcol(Z_VW)),
            zcol(Z_MISC, LANES)[:, None, :], ocmp_s, layer=l, n_past=n_past)
        outs = _out_sample(xs, zs, jnp.swapaxes(state_pool[l], 0, 1), state_conv[l, :, 0], state_conv[l, :, 1],
                           olat_s.reshape(dec, 512), ynsa_s.reshape(dec, 512), lw, final_gain,
                           n_past=n_past, final=final)
        if final:
            xs, ys, zc_s = outs
        else:
            xs, zc_s = outs
        app = lambda old, new: jnp.concatenate([old, new[:, None, :]], axis=1)[:, -old.shape[1]:]
        new_s.append((lat_s[:, None, :], kr_s[:, None, :MLA_ROPE],
                      zcol(Z_KC)[:, None], zcol(Z_VC)[:, None], zcol(Z_KS)[:, None], zcol(Z_VS)[:, None],
                      win_k_new, win_v_new,
                      app(state_conv[l], zc_s), app(state_pool[l], zcol(Z_POOL_U, D_GROUP))))
    stack = lambda rows: tuple(jnp.stack([r[i] for r in rows]) for i in range(len(rows[0])))
    return ((yp.reshape(batch, seq, D_MODEL), ys.reshape(dec, 1, D_MODEL)) + stack(new_p) + stack(new_s))
```

```python
import functools

import numpy as np
import jax
import jax.numpy as jnp
from jax import lax
from jax.experimental import pallas as pl
from jax.experimental.pallas import tpu as pltpu

F32 = jnp.float32
BF16 = jnp.bfloat16

D_MODEL = 2048
D_GROUP = 512
POOL_WINDOWS = (2, 4, 8, 16)
POOL_CH = 128
POOL_HIST = 15
MLA_HEADS = 4
MLA_NOPE = 128
MLA_ROPE = 32
MLA_Q_LORA = 384
MLA_KV_LORA = 128
ROPE_THETA = 10000.0
MLA_SCALE = (MLA_NOPE + MLA_ROPE) ** -0.5
CONV_WIDTH = 3
NSA_HEADS = 4
NSA_DK = 128
NSA_L_CMP = 32
NSA_L_SEL = 64
NSA_TOPN = 16
NSA_WINDOW = 512
NSA_SCALE = NSA_DK ** -0.5
FORCED_SCORE = 1e6
RMS_EPS = 1e-6
NEG_INF = -1e30
M_INIT = -5e29
PAGE_SIZE = 128
ALIBI = tuple(2.0 ** (-8.0 * (h + 1) / NSA_HEADS) for h in range(NSA_HEADS))

LANES = 128
Q_BLOCK = 128
K_BLOCK = 512
VMEM_LIMIT = 56 * 1024 * 1024

Z_POOL_U, Z_POOL_G = 0, 512
Z_CQ, Z_CKV = 1024, 1408
Z_MLA_G = 1536
Z_CONV_H, Z_CONV_C, Z_CONV_B, Z_CONV_G = 2048, 2560, 3072, 3584
Z_NSA_Q, Z_NSA_G = 4096, 4608
Z_KC, Z_VC, Z_KS, Z_VS, Z_KW, Z_VW = 5120, 5248, 5376, 5504, 5632, 5760
Z_MISC = 5888
Z_WIDTH = 6144
MISC_GATE0 = 32


def _nn(a, b):
    return jnp.dot(a, b, preferred_element_type=F32)


def _nt(a, b):
    return lax.dot_general(a, b, (((1,), (1,)), ((), ())), preferred_element_type=F32)


def _tn(a, b):
    return lax.dot_general(a, b, (((0,), (0,)), ((), ())), preferred_element_type=F32)


def _rms(x, g):
    return x * lax.rsqrt(jnp.mean(x * x, axis=-1, keepdims=True) + RMS_EPS) * g


def _params(n_axes):
    return pltpu.CompilerParams(dimension_semantics=("arbitrary",) * n_axes,
                                vmem_limit_bytes=VMEM_LIMIT)


W_KROPE, W_MLA_G, W_KC, W_BG, W_NSA_G, W_END = 1536, 1568, 4640, 5408, 5420, 5932


def _win_layout_kernel(w_ref, o_ref):
    x = w_ref[...]
    rows = x.shape[0]
    seg = lambda a, b: x[:, a:b].astype(BF16)
    o_ref[:, :Z_MLA_G] = seg(0, W_KROPE)
    o_ref[:, Z_MLA_G:Z_NSA_G] = seg(W_MLA_G, W_KC)
    o_ref[:, Z_NSA_G:Z_KC] = seg(W_NSA_G, W_END)
    o_ref[:, Z_KC:Z_MISC] = seg(W_KC, W_BG)
    n_gate = W_NSA_G - W_BG
    o_ref[:, Z_MISC:Z_MISC + LANES] = jnp.concatenate(
        [seg(W_KROPE, W_MLA_G), seg(W_BG, W_NSA_G), jnp.zeros((rows, LANES - MLA_ROPE - n_gate), BF16)], axis=1)
    o_ref[:, Z_MISC + LANES:] = jnp.zeros((rows, Z_WIDTH - Z_MISC - LANES), BF16)


def _win_layout(w_in, layer, *, tr):
    _, d, n = w_in.shape
    return pl.pallas_call(
        _win_layout_kernel,
        grid=(d // tr,),
        in_specs=[pl.BlockSpec((None, tr, n), lambda i: (layer, i, 0))],
        out_specs=pl.BlockSpec((tr, Z_WIDTH), lambda i: (i, 0)),
        out_shape=jax.ShapeDtypeStruct((d, Z_WIDTH), BF16),
        compiler_params=_params(1),
        name="win_layout",
    )(w_in)


def _inproj_kernel(x_ref, g_ref, w_ref, o_ref, xn_ref, *, tm):
    @pl.when(pl.program_id(1) == 0)
    def _():
        rows = 64 if tm % 64 == 0 else tm

        def body(i, c):
            r = pl.multiple_of(i * rows, rows)
            xn_ref[pl.ds(r, rows), :] = _rms(x_ref[pl.ds(r, rows), :], g_ref[...]).astype(BF16)
            return c
        lax.fori_loop(0, tm // rows, body, 0)

    o_ref[...] = _nn(xn_ref[...], w_ref[...])


def _inproj(x, g, w, *, tm, tn):
    m = x.shape[0]
    return pl.pallas_call(
        functools.partial(_inproj_kernel, tm=tm),
        grid=(m // tm, Z_WIDTH // tn),
        in_specs=[pl.BlockSpec((tm, D_MODEL), lambda i, j: (i, 0)),
                  pl.BlockSpec((1, D_MODEL), lambda i, j: (0, 0)),
                  pl.BlockSpec((D_MODEL, tn), lambda i, j: (0, j))],
        out_specs=pl.BlockSpec((tm, tn), lambda i, j: (i, j)),
        out_shape=jax.ShapeDtypeStruct((m, Z_WIDTH), F32),
        scratch_shapes=[pltpu.VMEM((tm, D_MODEL), BF16)],
        compiler_params=_params(2),
        name="inproj",
    )(x, g, w)


def _rope128(x, cos, sin, lane):
    sw = jnp.where((lane & (MLA_ROPE - 1)) < MLA_ROPE // 2, pltpu.roll(x, LANES - MLA_ROPE // 2, 1),
                   pltpu.roll(x, MLA_ROPE // 2, 1))
    return x * cos + sw * sin


def _prep_kernel(zq_ref, zn_ref, cos_ref, sin_ref, qg_ref, kg_ref, wqn_ref, wqr_ref, wuk_ref,
                 gk_ref, gv_ref, *out_refs, tb, with_nsa):
    if with_nsa:
        qcat_ref, kcat_ref, lat_ref, kr_ref, kvb_ref, kc_ref, vc_ref = out_refs
    else:
        qcat_ref, kcat_ref, lat_ref, kr_ref = out_refs
    zq = zq_ref[...]
    cos = cos_ref[...]
    sin = sin_ref[...]
    lane = lax.broadcasted_iota(jnp.int32, (tb, LANES), 1)
    cqn = _rms(zq[:, :MLA_Q_LORA], qg_ref[...]).astype(BF16)
    lat = _rms(zq[:, MLA_Q_LORA:], kg_ref[...])
    q_nope = _nn(cqn, wqn_ref[...])
    q_rope = _rope128(_nn(cqn, wqr_ref[...]), cos, sin, lane) * MLA_SCALE
    misc = zn_ref[:, Z_MISC - Z_KC:Z_MISC - Z_KC + LANES]
    kr = jnp.where(lane < MLA_ROPE, _rope128(misc, cos, sin, lane), 0.0)
    for h in range(MLA_HEADS):
        q_lat = _nn(q_nope[:, h * MLA_NOPE:(h + 1) * MLA_NOPE].astype(BF16), wuk_ref[h]) * MLA_SCALE
        qcat_ref[:, h * 256:h * 256 + LANES] = q_lat.astype(BF16)
        qr = q_rope if h == 0 else pltpu.roll(q_rope, LANES - MLA_ROPE * h, 1)
        qcat_ref[:, h * 256 + LANES:(h + 1) * 256] = jnp.where(lane < MLA_ROPE, qr, 0.0).astype(BF16)
    kcat_ref[:, :LANES] = lat.astype(BF16)
    kcat_ref[:, LANES:] = kr.astype(BF16)
    lat_ref[...] = lat
    kr_ref[...] = kr
    if with_nsa:
        kvb_ref[...] = zn_ref[:, Z_KS - Z_KC:Z_VW - Z_KC + LANES].astype(BF16)
        nb = tb // NSA_L_CMP
        kcb = zn_ref[:, 0:LANES].reshape(nb, NSA_L_CMP, LANES) * gk_ref[...][None]
        vcb = zn_ref[:, LANES:2 * LANES].reshape(nb, NSA_L_CMP, LANES) * gv_ref[...][None]
        kc_ref[...] = jnp.sum(kcb, axis=1) * (1.0 / NSA_L_CMP)
        vc_ref[...] = jnp.sum(vcb, axis=1) * (1.0 / NSA_L_CMP)


def _prep(z, cos, sin, lw, *, tb, rows_per_table, with_nsa):
    m = z.shape[0]
    nt = rows_per_table // tb
    row = lambda i: (i, 0)
    full2 = lambda i: (0, 0)
    in_specs = [pl.BlockSpec((tb, 512), lambda i: (i, Z_CQ // 512)),
                pl.BlockSpec((tb, 1024), lambda i: (i, Z_KC // 1024)),
                pl.BlockSpec((tb, LANES), lambda i: (i % nt, 0)),
                pl.BlockSpec((tb, LANES), lambda i: (i % nt, 0)),
                pl.BlockSpec((1, MLA_Q_LORA), full2),
                pl.BlockSpec((1, MLA_KV_LORA), full2),
                pl.BlockSpec((MLA_Q_LORA, MLA_HEADS * MLA_NOPE), full2),
                pl.BlockSpec((MLA_Q_LORA, MLA_HEADS * MLA_ROPE), full2),
                pl.BlockSpec((MLA_HEADS, MLA_NOPE, MLA_KV_LORA), lambda i: (0, 0, 0)),
                pl.BlockSpec((NSA_L_CMP, NSA_DK), full2),
                pl.BlockSpec((NSA_L_CMP, NSA_DK), full2)]
    out_specs = [pl.BlockSpec((tb, 1024), row), pl.BlockSpec((tb, 256), row),
                 pl.BlockSpec((tb, LANES), row), pl.BlockSpec((tb, LANES), row)]
    out_shape = [jax.ShapeDtypeStruct((m, 1024), BF16), jax.ShapeDtypeStruct((m, 256), BF16),
                 jax.ShapeDtypeStruct((m, LANES), F32), jax.ShapeDtypeStruct((m, LANES), F32)]
    if with_nsa:
        nb = tb // NSA_L_CMP
        out_specs += [pl.BlockSpec((tb, 512), row), pl.BlockSpec((nb, LANES), row), pl.BlockSpec((nb, LANES), row)]
        out_shape += [jax.ShapeDtypeStruct((m, 512), BF16),
                      jax.ShapeDtypeStruct((m // NSA_L_CMP, LANES), F32),
                      jax.ShapeDtypeStruct((m // NSA_L_CMP, LANES), F32)]
    return pl.pallas_call(
        functools.partial(_prep_kernel, tb=tb, with_nsa=with_nsa),
        grid=(m // tb,),
        in_specs=in_specs, out_specs=out_specs, out_shape=out_shape,
        compiler_params=_params(1),
        name="prep_nsa" if with_nsa else "prep",
    )(z, z, cos, sin, lw["q_norm"], lw["kv_norm"], lw["w_uq_nope"], lw["w_uq_rope"], lw["w_ukT"],
      lw["gain_k"], lw["gain_v"])


def _online_step(s, v, m_ref, l_ref, acc_ref):
    tiles = [s[:, i * LANES:(i + 1) * LANES] for i in range(s.shape[1] // LANES)]
    m_prev = m_ref[...]
    m_new = jnp.maximum(m_prev, jnp.max(functools.reduce(jnp.maximum, tiles), axis=1, keepdims=True))
    alpha = jnp.exp(m_prev - m_new)
    ps = [jnp.exp(t - m_new) for t in tiles]
    l_ref[...] = alpha * l_ref[...] + jnp.sum(functools.reduce(jnp.add, ps), axis=1, keepdims=True)
    acc_ref[...] = alpha * acc_ref[...] + _nn(jnp.concatenate(ps, axis=1).astype(BF16), v)
    m_ref[...] = m_new


def _reset_state(m_ref, l_ref, acc_ref):
    m_ref[...] = jnp.full(m_ref.shape, M_INIT, F32)
    l_ref[...] = jnp.zeros(l_ref.shape, F32)
    acc_ref[...] = jnp.zeros(acc_ref.shape, F32)


def _stack_heads(x):
    return jnp.concatenate([x] * NSA_HEADS, axis=0)


def _mla_prompt_kernel(q_ref, k_ref, o_ref, m_ref, l_ref, acc_ref):
    qi = pl.program_id(1)
    _reset_state(m_ref, l_ref, acc_ref)
    q_all = jnp.concatenate([q_ref[:, h * 256:(h + 1) * 256] for h in range(MLA_HEADS)], axis=0)
    rc = (lax.broadcasted_iota(jnp.int32, (Q_BLOCK, K_BLOCK), 0)
          - lax.broadcasted_iota(jnp.int32, (Q_BLOCK, K_BLOCK), 1))

    def body(kb, carry):
        k0 = pl.multiple_of(kb * K_BLOCK, K_BLOCK)
        k = k_ref[pl.ds(k0, K_BLOCK), :]
        pen = jnp.where(rc + (qi * Q_BLOCK - kb * K_BLOCK) >= 0, 0.0, NEG_INF)
        _online_step(_nt(q_all, k) + _stack_heads(pen), k[:, :MLA_KV_LORA], m_ref, l_ref, acc_ref)
        return carry

    lax.fori_loop(0, (qi * Q_BLOCK) // K_BLOCK + 1, body, 0)
    out = acc_ref[...] / jnp.maximum(l_ref[...], 1e-30)
    for h in range(MLA_HEADS):
        o_ref[:, h * LANES:(h + 1) * LANES] = out[h * Q_BLOCK:(h + 1) * Q_BLOCK]


def _mla_prompt(qcat, kcat, *, batch, seq):
    nq = seq // Q_BLOCK
    return pl.pallas_call(
        _mla_prompt_kernel,
        grid=(batch, nq),
        in_specs=[pl.BlockSpec((Q_BLOCK, 1024), lambda b, i: (b * nq + i, 0)),
                  pl.BlockSpec((seq, 256), lambda b, i: (b, 0))],
        out_specs=pl.BlockSpec((Q_BLOCK, 512), lambda b, i: (b * nq + i, 0)),
        out_shape=jax.ShapeDtypeStruct((batch * seq, 512), F32),
        scratch_shapes=[pltpu.VMEM((MLA_HEADS * Q_BLOCK, LANES), F32)] * 3,
        compiler_params=_params(2),
        name="mla_prompt",
    )(qcat, kcat)


def _nsa_prompt_kernel(q_ref, misc_ref, kc_ref, vc_ref, kvb_ref, e_ref, o_ref,
                       selmask_ref, m_ref, l_ref, acc_ref, *, seq, n_sel):
    qi = pl.program_id(1)
    t0 = qi * Q_BLOCK
    nc = seq // NSA_L_CMP
    q = q_ref[...] * NSA_SCALE
    qh = [q[:, h * NSA_DK:(h + 1) * NSA_DK].astype(BF16) for h in range(NSA_HEADS)]
    gates = jax.nn.sigmoid(misc_ref[...])

    def gate(j, h):
        c = MISC_GATE0 + NSA_HEADS * j + h
        return gates[:, c:c + 1]

    kc = kc_ref[...].astype(BF16)
    vc = vc_ref[...].astype(BF16)
    q_all = jnp.concatenate(qh, axis=0)
    n_all = NSA_HEADS * Q_BLOCK
    q_shift = Q_BLOCK.bit_length() - 1

    def geometry(shape, q_axis):
        qidx = lax.broadcasted_iota(jnp.int32, shape, q_axis)
        t = t0 + (qidx & (Q_BLOCK - 1))
        bend = lax.broadcasted_iota(jnp.int32, shape, 1 - q_axis) * NSA_L_CMP + (NSA_L_CMP - 1)
        slope = functools.reduce(lambda a, h: jnp.where((qidx >> q_shift) == h, ALIBI[h], a), range(NSA_HEADS),
                                 jnp.zeros(shape, F32))
        return bend <= t, slope * (t - bend).astype(F32)

    mask_a, pen_a = geometry((n_all, nc), 0)
    s_a = jnp.where(mask_a, _nt(q_all, kc) - pen_a, NEG_INF)
    e_a = jnp.where(mask_a, jnp.exp(s_a - jnp.max(s_a, axis=1, keepdims=True)), 0.0)
    p_a = e_a / jnp.maximum(jnp.sum(e_a, axis=1, keepdims=True), 1e-30)
    o_cmp = _nn(p_a.astype(BF16), vc)
    for h in range(NSA_HEADS):
        o_ref[:, h * NSA_DK:(h + 1) * NSA_DK] = gate(0, h) * o_cmp[h * Q_BLOCK:(h + 1) * Q_BLOCK]
    mask_b, pen_b = geometry((nc, n_all), 1)
    s_b = jnp.where(mask_b, _nt(kc, q_all) - pen_b, NEG_INF)
    e_b = jnp.where(mask_b, jnp.exp(s_b - jnp.max(s_b, axis=0, keepdims=True)), 0.0)
    p_b = e_b / jnp.maximum(jnp.sum(e_b, axis=0, keepdims=True), 1e-30)
    imp = functools.reduce(jnp.add, [p_b[:, h * Q_BLOCK:(h + 1) * Q_BLOCK] for h in range(NSA_HEADS)])
    t_b = t0 + lax.broadcasted_iota(jnp.int32, (nc, Q_BLOCK), 1)

    imp2 = imp + pltpu.roll(imp, nc - 1, 0)
    ridx = lax.broadcasted_iota(jnp.int32, (nc, Q_BLOCK), 0)
    blk = ridx >> 1
    cur = t_b >> 6
    forced = (blk == 0) | (blk == cur) | (blk == cur - 1)
    score = jnp.where(blk <= cur, jnp.where(forced, FORCED_SCORE, imp2), -FORCED_SCORE)
    rank = jnp.zeros((nc, Q_BLOCK), F32)
    for i in range(0, nc, 2):
        s_i = score[i:i + 1, :]
        ahead = (s_i > score) | ((s_i == score) & (ridx > i))
        rank = rank + jnp.where(ahead, 1.0, 0.0)
    sel_t = jnp.where((rank < n_sel) & ((ridx & 1) == 0), 1.0, 0.0).astype(BF16)
    selmask_ref[...] = (_tn(sel_t, e_ref[...]) - 1.0) * (-NEG_INF)

    rc =(lax.broadcasted_iota(jnp.int32, (Q_BLOCK, K_BLOCK), 0)
          - lax.broadcasted_iota(jnp.int32, (Q_BLOCK, K_BLOCK), 1))

    def branch(first_kb, col_k, col_v, j, use_sel):
        _reset_state(m_ref, l_ref, acc_ref)

        def body(kb, carry):
            k0 = pl.multiple_of(kb * K_BLOCK, K_BLOCK)
            k = kvb_ref[pl.ds(k0, K_BLOCK), col_k:col_k + NSA_DK]
            v = kvb_ref[pl.ds(k0, K_BLOCK), col_v:col_v + NSA_DK]
            dist_i = rc + (qi * Q_BLOCK - kb * K_BLOCK)
            dist = dist_i.astype(F32)
            if use_sel:
                base = jnp.where(dist_i >= 0, 0.0, NEG_INF) + selmask_ref[:, pl.ds(k0, K_BLOCK)]
            else:
                base = jnp.where(dist_i >= 0, jnp.where(dist_i <= NSA_WINDOW, 0.0, NEG_INF), NEG_INF)
            pen = jnp.concatenate([base - ALIBI[h] * dist for h in range(NSA_HEADS)], axis=0)
            _online_step(_nt(q_all, k) + pen, v, m_ref, l_ref, acc_ref)
            return carry

        lax.fori_loop(first_kb, (qi * Q_BLOCK) // K_BLOCK + 1, body, 0)
        out = acc_ref[...] / jnp.maximum(l_ref[...], 1e-30)
        for h in range(NSA_HEADS):
            o_ref[:, h * NSA_DK:(h + 1) * NSA_DK] += gate(j, h) * out[h * Q_BLOCK:(h + 1) * Q_BLOCK]

    branch(0, 0, NSA_DK, 1, True)
    branch(jnp.maximum(qi * Q_BLOCK - NSA_WINDOW, 0) // K_BLOCK, 2 * NSA_DK, 3 * NSA_DK, 2, False)


def _nsa_prompt(z, kc, vc, kvb, *, batch, seq):
    nq = seq // Q_BLOCK
    nc = seq // NSA_L_CMP
    n_sb = -(-seq // NSA_L_SEL)
    expand = np.zeros((nc, seq), np.float32)
    for j in range(n_sb):
        expand[2 * j, j * NSA_L_SEL:(j + 1) * NSA_L_SEL] = 1.0
    return pl.pallas_call(
        functools.partial(_nsa_prompt_kernel, seq=seq, n_sel=min(NSA_TOPN, n_sb)),
        grid=(batch, nq),
        in_specs=[pl.BlockSpec((Q_BLOCK, 512), lambda b, i: (b * nq + i, Z_NSA_Q // 512)),
                  pl.BlockSpec((Q_BLOCK, LANES), lambda b, i: (b * nq + i, Z_MISC // LANES)),
                  pl.BlockSpec((nc, LANES), lambda b, i: (b, 0)),
                  pl.BlockSpec((nc, LANES), lambda b, i: (b, 0)),
                  pl.BlockSpec((seq, 512), lambda b, i: (b, 0)),
                  pl.BlockSpec((nc, seq), lambda b, i: (0, 0))],
        out_specs=pl.BlockSpec((Q_BLOCK, 512), lambda b, i: (b * nq + i, 0)),
        out_shape=jax.ShapeDtypeStruct((batch * seq, 512), F32),
        scratch_shapes=[pltpu.VMEM((Q_BLOCK, seq), F32)] + [pltpu.VMEM((NSA_HEADS * Q_BLOCK, LANES), F32)] * 3,
        compiler_params=_params(2),
        name="nsa_prompt",
    )(z, z, kc, vc, kvb, jnp.asarray(expand, BF16))


def _silu(x):
    return x * jax.nn.sigmoid(x)


def _mix_and_project(x, y_pool, olat, y_conv, y_nsa, g_pool, g_mla, g_conv, g_nsa, wuv_ref, wout_ref):
    y_mla = jnp.concatenate(
        [_nn(olat[:, h * MLA_KV_LORA:(h + 1) * MLA_KV_LORA].astype(BF16), wuv_ref[h]) for h in range(MLA_HEADS)],
        axis=1)
    mix = jnp.concatenate([y_pool * _silu(g_pool), y_mla * _silu(g_mla),
                           y_conv * _silu(g_conv), y_nsa * _silu(g_nsa)], axis=1)
    return x + _nn(mix.astype(BF16), wout_ref[...])


def _pool_project(d, pw_ref, ps_ref):
    return jnp.concatenate(
        [_nn(d[:, g * POOL_CH:(g + 1) * POOL_CH].astype(BF16), pw_ref[g]) for g in range(len(POOL_WINDOWS))],
        axis=1) * ps_ref[...]


def _out_prompt_kernel(x_ref, u_ref, uh_ref, gp_ref, gm_ref, ch_ref, cc_ref, cb_ref, gc_ref, gn_ref,
                       chh_ref, cch_ref, olat_ref, ynsa_ref, pw_ref, ps_ref, wuv_ref, cw_ref, cbias_ref,
                       wout_ref, fg_ref, *out_refs, tb, seq, final):
    if final:
        xo_ref, y_ref, zc_ref = out_refs
    else:
        xo_ref, zc_ref = out_refs
    i = pl.program_id(0)
    t0 = (i * tb) % seq
    first = t0 == 0
    u = u_ref[...]
    full = jnp.concatenate([jnp.where(first, 0.0, uh_ref[...]), u], axis=0)
    t = (t0 + lax.broadcasted_iota(jnp.int32, (tb, POOL_CH), 0) + 1).astype(F32)
    sums = full
    means = []
    step = 1
    for g, w in enumerate(POOL_WINDOWS):
        while step < w:
            sums = sums + pltpu.roll(sums, step, 0)
            step *= 2
        sg = sums[:, g * POOL_CH:(g + 1) * POOL_CH]
        means.append(sg[POOL_HIST + 1:, :] / jnp.minimum(float(w), t))
    d = jnp.concatenate(means, axis=1) - u
    y_pool = _pool_project(d, pw_ref, ps_ref)
    zc = cc_ref[...] * ch_ref[...]
    zfull = jnp.concatenate([jnp.where(first, 0.0, cch_ref[...] * chh_ref[...]), zc], axis=0)
    cw = cw_ref[...]
    conv = (cw[0:1] * pltpu.roll(zfull, 2, 0)[8:] + cw[1:2] * pltpu.roll(zfull, 1, 0)[8:]
            + cw[2:3] * zc + cbias_ref[...])
    y_conv = cb_ref[...] * conv
    zc_ref[...] = zc[tb - 8:, :]
    out = _mix_and_project(x_ref[...], y_pool, olat_ref[...], y_conv, ynsa_ref[...],
                           gp_ref[...], gm_ref[...], gc_ref[...], gn_ref[...], wuv_ref, wout_ref)
    xo_ref[...] = out
    if final:
        y_ref[...] = _rms(out, fg_ref[...])


def _out_prompt(x, z, olat, ynsa, lw, final_gain, *, seq, tb, final):
    m = x.shape[0]
    zblk = lambda c: pl.BlockSpec((tb, 512), lambda i: (i, c // 512))
    halo16 = pl.BlockSpec((16, 512), lambda i: (jnp.maximum(i * (tb // 16) - 1, 0), Z_POOL_U // 512))
    halo8 = lambda c: pl.BlockSpec((8, 512), lambda i: (jnp.maximum(i * (tb // 8) - 1, 0), c // 512))
    row = lambda w: pl.BlockSpec((tb, w), lambda i: (i, 0))
    cst2 = lambda a, b: pl.BlockSpec((a, b), lambda i: (0, 0))
    cst3 = lambda a, b, c: pl.BlockSpec((a, b, c), lambda i: (0, 0, 0))
    in_specs = [row(D_MODEL), zblk(Z_POOL_U), halo16, zblk(Z_POOL_G), zblk(Z_MLA_G),
                zblk(Z_CONV_H), zblk(Z_CONV_C), zblk(Z_CONV_B), zblk(Z_CONV_G), zblk(Z_NSA_G),
                halo8(Z_CONV_H), halo8(Z_CONV_C), row(512), row(512),
                cst3(len(POOL_WINDOWS), POOL_CH, POOL_CH), cst2(1, D_GROUP),
                cst3(MLA_HEADS, MLA_KV_LORA, 128), cst2(CONV_WIDTH, D_GROUP), cst2(1, D_GROUP),
                cst2(D_MODEL, D_MODEL), cst2(1, D_MODEL)]
    out_specs = [row(D_MODEL)] + ([row(D_MODEL)] if final else []) + [pl.BlockSpec((8, 512), lambda i: (i, 0))]
    out_shape = ([jax.ShapeDtypeStruct((m, D_MODEL), F32)] * (2 if final else 1)
                 + [jax.ShapeDtypeStruct((m // tb * 8, 512), F32)])
    return pl.pallas_call(
        functools.partial(_out_prompt_kernel, tb=tb, seq=seq, final=final),
        grid=(m // tb,),
        in_specs=in_specs, out_specs=out_specs, out_shape=out_shape,
        compiler_params=_params(1),
        name="out_prompt_final" if final else "out_prompt",
    )(x, z, z, z, z, z, z, z, z, z, z, z, olat, ynsa, lw["pool_w"], lw["pool_scale"], lw["w_uv"],
      lw["conv_w"], lw["conv_b"], lw["w_out"], final_gain)


def _out_sample_kernel(x_ref, z_ref, pst_ref, c0_ref, c1_ref, olat_ref, ynsa_ref, pw_ref, ps_ref, wuv_ref,
                       cw_ref, cbias_ref, wout_ref, fg_ref, *out_refs, n_past, final):
    if final:
        xo_ref, y_ref, zc_ref = out_refs
    else:
        xo_ref, zc_ref = out_refs
    zcol = lambda c: z_ref[:, c:c + 512]
    u = zcol(Z_POOL_U)
    sums = u
    means = []
    back = 1
    for g, w in enumerate(POOL_WINDOWS):
        while back < w:
            sums = sums + pst_ref[POOL_HIST - back]
            back += 1
        means.append(sums[:, g * POOL_CH:(g + 1) * POOL_CH] / float(min(w, n_past + 1)))
    d = jnp.concatenate(means, axis=1) - u
    y_pool = _pool_project(d, pw_ref, ps_ref)
    zc = zcol(Z_CONV_C) * zcol(Z_CONV_H)
    cw = cw_ref[...]
    conv = cw[0:1] * c0_ref[...] + cw[1:2] * c1_ref[...] + cw[2:3] * zc + cbias_ref[...]
    y_conv = zcol(Z_CONV_B) * conv
    zc_ref[...] = zc
    out = _mix_and_project(x_ref[...], y_pool, olat_ref[...], y_conv, ynsa_ref[...],
                           zcol(Z_POOL_G), zcol(Z_MLA_G), zcol(Z_CONV_G), zcol(Z_NSA_G), wuv_ref, wout_ref)
    xo_ref[...] = out
    if final:
        y_ref[...] = _rms(out, fg_ref[...])


def _out_sample(x, z, pool_state_t, conv0, conv1, olat, ynsa, lw, final_gain, *, n_past, final):
    m = x.shape[0]
    n_out = 3 if final else 2
    out_shape = ([jax.ShapeDtypeStruct((m, D_MODEL), F32)] * (n_out - 1) + [jax.ShapeDtypeStruct((m, 512), F32)])
    return pl.pallas_call(
        functools.partial(_out_sample_kernel, n_past=n_past, final=final),
        out_shape=out_shape,
        compiler_params=pltpu.CompilerParams(vmem_limit_bytes=VMEM_LIMIT),
        name="out_sample_final" if final else "out_sample",
    )(x, z, pool_state_t, conv0, conv1, olat, ynsa, lw["pool_w"], lw["pool_scale"], lw["w_uv"],
      lw["conv_w"], lw["conv_b"], lw["w_out"], final_gain)


HEAD_ROWS = 16
PAGE_SLOTS = 3


def _rows_to_row(o):
    return jnp.concatenate([o[h:h + 1, :] for h in range(NSA_HEADS)], axis=1)


def _sample_paged_kernel(pt_ref, lat_hbm, kr_hbm, ck_hbm, cv_hbm, qm_ref, qr_ref, qn_ref, latn_ref, krn_ref,
                         gk_ref, gv_ref, olat_ref, ocmp_ref, imp_ref,
                         lat_buf, kr_buf, ck_buf, cv_buf, sem, m_ref, l_ref, acc_ref, kc_ref, vc_ref,
                         *, layer, cp, n_chunks, n_past, batch):
    b = pl.program_id(0)
    krows = MLA_ROPE

    def copies(row, chunk, slot):
        out = []
        for i in range(cp):
            page = pt_ref[row, chunk * cp + i]
            out.append(pltpu.make_async_copy(lat_hbm.at[layer, page], lat_buf.at[slot, pl.ds(i * PAGE_SIZE, PAGE_SIZE)],
                                             sem.at[slot]))
            out.append(pltpu.make_async_copy(kr_hbm.at[layer, page], kr_buf.at[slot, pl.ds(i * krows, krows)],
                                             sem.at[slot]))
            out.append(pltpu.make_async_copy(ck_hbm.at[layer, page], ck_buf.at[slot, pl.ds(i * PAGE_SIZE, PAGE_SIZE)],
                                             sem.at[slot]))
            out.append(pltpu.make_async_copy(cv_hbm.at[layer, page], cv_buf.at[slot, pl.ds(i * PAGE_SIZE, PAGE_SIZE)],
                                             sem.at[slot]))
        return out

    total = batch * n_chunks

    def chunk_copies(g):
        gw = lax.rem(g, total)
        return copies(gw // n_chunks, lax.rem(gw, n_chunks), lax.rem(g, PAGE_SLOTS))

    @pl.when(b == 0)
    def _():
        for g0 in range(PAGE_SLOTS - 1):
            for cpy in chunk_copies(g0):
                cpy.start()

    m_ref[...] = jnp.full(m_ref.shape, M_INIT, F32)
    l_ref[...] = jnp.zeros(l_ref.shape, F32)
    acc_ref[...] = jnp.zeros(acc_ref.shape, F32)
    qm = qm_ref[...]
    qr = qr_ref[:, :MLA_ROPE]
    gk = gk_ref[...][None]
    gv = gv_ref[...][None]
    bpc = cp * PAGE_SIZE // NSA_L_CMP

    def chunk_body(c, carry):
        g = b * n_chunks + c
        slot = lax.rem(g, PAGE_SLOTS)
        for cpy in copies(b, c, slot):
            cpy.wait()
        for cpy in chunk_copies(g + PAGE_SLOTS - 1):
            cpy.start()

        lat = lat_buf[slot].astype(BF16)
        kr = kr_buf[slot].astype(BF16)
        s_rope = jnp.concatenate([_nn(qr, kr[i * krows:(i + 1) * krows, :]) for i in range(cp)], axis=1)
        s = _nt(qm, lat) + s_rope
        m_prev = m_ref[...]
        m_new = jnp.maximum(m_prev, jnp.max(s, axis=1, keepdims=True))
        alpha = jnp.exp(m_prev - m_new)
        p = jnp.exp(s - m_new[:, 0:1])
        l_ref[...] = alpha * l_ref[...] + jnp.sum(p, axis=1, keepdims=True)
        acc_ref[...] = alpha * acc_ref[...] + _nn(p.astype(BF16), lat)
        m_ref[...] = m_new
        r0 = pl.multiple_of(c * bpc, bpc)
        kc_ref[pl.ds(r0, bpc), :] = jnp.sum(ck_buf[slot].reshape(bpc, NSA_L_CMP, NSA_DK) * gk, axis=1) * (1.0 / NSA_L_CMP)
        vc_ref[pl.ds(r0, bpc), :] = jnp.sum(cv_buf[slot].reshape(bpc, NSA_L_CMP, NSA_DK) * gv, axis=1) * (1.0 / NSA_L_CMP)
        return carry

    lax.fori_loop(0, n_chunks, chunk_body, 0)

    @pl.when(b == batch - 1)
    def _():
        for g0 in range(PAGE_SLOTS - 1):
            for cpy in chunk_copies(total + g0):
                cpy.wait()

    latn = latn_ref[...].astype(BF16)
    is_new = lax.broadcasted_iota(jnp.int32, (HEAD_ROWS, HEAD_ROWS), 1) == 0
    s_n = jnp.where(is_new, _nt(qm, latn) + _nt(qr, krn_ref[:, :MLA_ROPE].astype(BF16)), NEG_INF)
    m_prev = m_ref[...]
    m_f = jnp.maximum(m_prev, jnp.max(s_n, axis=1, keepdims=True))
    a_f = jnp.exp(m_prev - m_f)
    p_n = jnp.where(is_new, jnp.exp(s_n - m_f[:, 0:HEAD_ROWS]), 0.0)
    l_f = l_ref[...] * a_f + jnp.sum(p_n, axis=1, keepdims=True)
    acc_f = acc_ref[...] * a_f + _nn(p_n.astype(BF16), latn)
    olat_ref[...] = _rows_to_row(acc_f / jnp.maximum(l_f, 1e-30))

    nblk = n_past // NSA_L_CMP
    s = _nt(qn_ref[...], kc_ref[...].astype(BF16))
    hrow = lax.broadcasted_iota(jnp.int32, (HEAD_ROWS, nblk), 0)
    bend = lax.broadcasted_iota(jnp.int32, (HEAD_ROWS, nblk), 1) * NSA_L_CMP + (NSA_L_CMP - 1)
    slope = functools.reduce(lambda a, h: jnp.where(hrow == h, ALIBI[h], a), range(NSA_HEADS),
                             jnp.zeros((HEAD_ROWS, nblk), F32))
    valid = bend <= n_past
    s = jnp.where(valid, s - slope * (n_past - bend).astype(F32), NEG_INF)
    e = jnp.where(valid, jnp.exp(s - jnp.max(s, axis=1, keepdims=True)), 0.0)
    p = e / jnp.maximum(jnp.sum(e, axis=1, keepdims=True), 1e-30)
    imp_ref[...] = jnp.sum(jnp.where(hrow < NSA_HEADS, p, 0.0), axis=0, keepdims=True)
    ocmp_ref[...] = _rows_to_row(_nn(p.astype(BF16), vc_ref[...].astype(BF16)))


def _sample_paged(page_table, c_lat, c_kr_t, c_ck, c_cv, qm, qr, qn, latn, krn, gk, gv, *, layer, cp):
    batch, n_pages = page_table.shape
    n_chunks = n_pages // cp
    n_past = n_pages * PAGE_SIZE
    nblk = n_past // NSA_L_CMP
    hbm = pl.BlockSpec(memory_space=pl.ANY)
    per_b3 = lambda r, w: pl.BlockSpec((None, r, w), lambda b, pt: (b, 0, 0))
    cst2 = lambda a, w: pl.BlockSpec((a, w), lambda b, pt: (0, 0))
    in_specs = [hbm, hbm, hbm, hbm,
                per_b3(HEAD_ROWS, LANES), per_b3(HEAD_ROWS, LANES), per_b3(HEAD_ROWS, LANES),
                per_b3(HEAD_ROWS, LANES), per_b3(HEAD_ROWS, LANES), cst2(NSA_L_CMP, NSA_DK), cst2(NSA_L_CMP, NSA_DK)]
    out_specs = [per_b3(1, 512), per_b3(1, 512), per_b3(1, nblk)]
    out_shape = [jax.ShapeDtypeStruct((batch, 1, 512), F32), jax.ShapeDtypeStruct((batch, 1, 512), F32),
                 jax.ShapeDtypeStruct((batch, 1, nblk), F32)]
    grid_spec = pltpu.PrefetchScalarGridSpec(
        num_scalar_prefetch=1, grid=(batch,), in_specs=in_specs, out_specs=out_specs,
        scratch_shapes=[pltpu.VMEM((PAGE_SLOTS, cp * PAGE_SIZE, MLA_KV_LORA), F32),
                        pltpu.VMEM((PAGE_SLOTS, cp * MLA_ROPE, PAGE_SIZE), F32),
                        pltpu.VMEM((PAGE_SLOTS, cp * PAGE_SIZE, NSA_DK), F32),
                        pltpu.VMEM((PAGE_SLOTS, cp * PAGE_SIZE, NSA_DK), F32),
                        pltpu.SemaphoreType.DMA((PAGE_SLOTS,)),
                        pltpu.VMEM((HEAD_ROWS, LANES), F32), pltpu.VMEM((HEAD_ROWS, LANES), F32),
                        pltpu.VMEM((HEAD_ROWS, MLA_KV_LORA), F32),
                        pltpu.VMEM((nblk, NSA_DK), F32), pltpu.VMEM((nblk, NSA_DK), F32)])
    return pl.pallas_call(
        functools.partial(_sample_paged_kernel, layer=layer, cp=cp, n_chunks=n_chunks, n_past=n_past, batch=batch),
        grid_spec=grid_spec, out_shape=out_shape,
        compiler_params=_params(1),
        name="sample_paged",
    )(page_table, c_lat, c_kr_t, c_ck, c_cv, qm, qr, qn, latn, krn, gk, gv)


def _select_kernel(imp_ref, idx_ref, *, n_past, n_sel):
    rows, nblk = imp_ref.shape
    ratio = NSA_L_SEL // NSA_L_CMP
    n_sb = -(-(n_past + 1) // NSA_L_SEL)
    n_in = nblk // ratio
    cur = n_past // NSA_L_SEL
    x = imp_ref[...]
    pair = x + pltpu.roll(x, nblk - 1, 1)
    lane = lax.broadcasted_iota(jnp.int32, (rows, nblk + LANES), 1)
    blk = jnp.where(lane < nblk, lane >> 1, n_in + lane - nblk)
    exists = ((lane < nblk) & ((lane & 1) == 0)) | ((lane >= nblk) & (blk < n_sb))
    imp = jnp.concatenate([pair, jnp.zeros((rows, LANES), F32)], axis=1)
    forced = (blk == 0) | (blk == cur) | (blk == cur - 1)
    score = jnp.where(blk * NSA_L_SEL <= n_past, jnp.where(forced, FORCED_SCORE, imp), -FORCED_SCORE)
    score = jnp.where(exists, score, -3e38)
    lane_f = lane.astype(F32)
    out_lane = lax.broadcasted_iota(jnp.int32, (rows, LANES), 1)
    out = jnp.zeros((rows, LANES), jnp.int32)
    for r in range(n_sel):
        top = jnp.max(score, axis=1, keepdims=True)
        first = jnp.min(jnp.where(score == top, lane_f, 1e9), axis=1, keepdims=True)
        first_i = first.astype(jnp.int32)
        chosen = jnp.where(first_i < nblk, first_i >> 1, n_in + first_i - nblk)
        out = jnp.where(out_lane == r, chosen, out)
        score = jnp.where(lane_f == first, -3e38, score)
    idx_ref[...] = out


def _select(imp, *, n_past, n_sel):
    rows = imp.shape[0]
    return pl.pallas_call(
        functools.partial(_select_kernel, n_past=n_past, n_sel=n_sel),
        out_shape=jax.ShapeDtypeStruct((rows, LANES), jnp.int32),
        compiler_params=pltpu.CompilerParams(vmem_limit_bytes=VMEM_LIMIT),
        name="select",
    )(imp)


def _sample_nsa_kernel(ph_ref, hf_ref, blk_ref, sk_hbm, sv_hbm, qn_ref, ksn_ref, vsn_ref, wk_ref, wv_ref,
                       kwn_ref, vwn_ref, misc_ref, ocmp_ref, o_ref, wko_ref, wvo_ref, k_buf, v_buf, sem,
                       *, layer, n_sel, n_past, wb, batch):
    b = pl.program_id(0)

    def copies(row, slot):
        out = []
        for i in range(n_sel):
            src = pl.ds(pl.multiple_of(hf_ref[row, i] * NSA_L_SEL, NSA_L_SEL), NSA_L_SEL)
            dst = pl.ds(i * NSA_L_SEL, NSA_L_SEL)
            out.append(pltpu.make_async_copy(sk_hbm.at[layer, ph_ref[row, i], src], k_buf.at[slot, dst], sem.at[slot]))
            out.append(pltpu.make_async_copy(sv_hbm.at[layer, ph_ref[row, i], src], v_buf.at[slot, dst], sem.at[slot]))
        return out

    @pl.when(b == 0)
    def _():
        for cpy in copies(0, 0):
            cpy.start()

    slot = b % 2

    @pl.when(b + 1 < batch)
    def _():
        for cpy in copies(b + 1, 1 - slot):
            cpy.start()

    for cpy in copies(b, slot):
        cpy.wait()
    qn = qn_ref[...]
    hcol = lax.broadcasted_iota(jnp.int32, (HEAD_ROWS, 1), 0)
    slope = functools.reduce(lambda a, h: jnp.where(hcol == h, ALIBI[h], a), range(NSA_HEADS),
                             jnp.zeros((HEAD_ROWS, 1), F32))
    is_new = lax.broadcasted_iota(jnp.int32, (HEAD_ROWS, HEAD_ROWS), 1) == 0

    def attend(keys, values, dists, k_new, v_new):
        scores = []
        for k, d in zip(keys, dists):
            scores.append(jnp.where(d >= 0, _nt(qn, k.astype(BF16)) - slope * d.astype(F32), NEG_INF))
        k_new = k_new.astype(BF16)
        s_n = jnp.where(is_new, _nt(qn, k_new), NEG_INF)
        m = functools.reduce(jnp.maximum, [jnp.max(s, axis=1, keepdims=True) for s in scores + [s_n]])
        p_n = jnp.where(is_new, jnp.exp(s_n - m), 0.0)
        lsum = jnp.sum(p_n, axis=1, keepdims=True)
        acc = _nn(p_n.astype(BF16), v_new.astype(BF16))
        for s, v, d in zip(scores, values, dists):
            p = jnp.where(d >= 0, jnp.exp(s - m), 0.0)
            lsum = lsum + jnp.sum(p, axis=1, keepdims=True)
            acc = acc + _nn(p.astype(BF16), v.astype(BF16))
        return _rows_to_row(acc / jnp.maximum(lsum, 1e-30))

    key = lax.broadcasted_iota(jnp.int32, (1, n_sel * NSA_L_SEL), 1)
    slot_of_key = key >> (NSA_L_SEL.bit_length() - 1)
    start = jnp.full((1, n_sel * NSA_L_SEL), -1, jnp.int32)
    for i in range(n_sel):
        blk = blk_ref[b, i]
        start = jnp.where(slot_of_key == i, jnp.where(blk < n_past // NSA_L_SEL, n_past - blk * NSA_L_SEL, -1), start)
    dist = start - (key & (NSA_L_SEL - 1))
    o_sel = attend([k_buf[slot]], [v_buf[slot]], [dist], ksn_ref[...], vsn_ref[...])
    wd = wb - lax.broadcasted_iota(jnp.int32, (1, wb), 1)
    wd = jnp.where(wd <= NSA_WINDOW, wd, -1)
    o_win = attend([wk_ref[...]], [wv_ref[...]], [wd], kwn_ref[...], vwn_ref[...])
    is_last = lax.broadcasted_iota(jnp.int32, (wb, NSA_DK), 0) == wb - 1
    wko_ref[...] = jnp.where(is_last, kwn_ref[0:1, :], pltpu.roll(wk_ref[...], wb - 1, 0))
    wvo_ref[...] = jnp.where(is_last, vwn_ref[0:1, :], pltpu.roll(wv_ref[...], wb - 1, 0))
    gates = jax.nn.sigmoid(misc_ref[...])
    o_cmp = ocmp_ref[...]
    pieces = []
    for h in range(NSA_HEADS):
        sl = slice(h * NSA_DK, (h + 1) * NSA_DK)
        g = [gates[:, MISC_GATE0 + NSA_HEADS * j + h:MISC_GATE0 + NSA_HEADS * j + h + 1] for j in range(3)]
        pieces.append(g[0] * o_cmp[:, sl] + g[1] * o_sel[:, sl] + g[2] * o_win[:, sl])
    o_ref[...] = jnp.concatenate(pieces, axis=1)


def _sample_nsa(phys, half, blk, c_sk, c_sv, qn, ksn, vsn, win_k, win_v, kwn, vwn, misc, ocmp, *,
                layer, n_past):
    batch, n_sel = blk.shape
    wb = win_k.shape[2]

    hbm = pl.BlockSpec(memory_space=pl.ANY)
    per_b3 = lambda r, w: pl.BlockSpec((None, r, w), lambda b, ph, hf, bl: (b, 0, 0))
    win = pl.BlockSpec((None, None, wb, NSA_DK), lambda b, ph, hf, bl: (layer, b, 0, 0))
    in_specs = [hbm, hbm, per_b3(HEAD_ROWS, LANES), per_b3(HEAD_ROWS, LANES), per_b3(HEAD_ROWS, LANES), win, win,
                per_b3(HEAD_ROWS, LANES), per_b3(HEAD_ROWS, LANES), per_b3(1, LANES), per_b3(1, 512)]
    grid_spec = pltpu.PrefetchScalarGridSpec(
        num_scalar_prefetch=3, grid=(batch,), in_specs=in_specs,
        out_specs=[per_b3(1, 512), per_b3(wb, NSA_DK), per_b3(wb, NSA_DK)],
        scratch_shapes=[pltpu.VMEM((2, n_sel * NSA_L_SEL, NSA_DK), F32),
                        pltpu.VMEM((2, n_sel * NSA_L_SEL, NSA_DK), F32),
                        pltpu.SemaphoreType.DMA((2,))])
    return pl.pallas_call(
        functools.partial(_sample_nsa_kernel, layer=layer, n_sel=n_sel, n_past=n_past, wb=wb, batch=batch),
        grid_spec=grid_spec,
        out_shape=[jax.ShapeDtypeStruct((batch, 1, 512), F32), jax.ShapeDtypeStruct((batch, wb, NSA_DK), F32),
                   jax.ShapeDtypeStruct((batch, wb, NSA_DK), F32)],
        compiler_params=_params(1),
        name="sample_nsa",
    )(phys, half, blk, c_sk, c_sv, qn, ksn, vsn, win_k, win_v, kwn, vwn, misc, ocmp)


def _rope_tables(pos):
    half = MLA_ROPE // 2
    inv = ROPE_THETA ** (-jnp.arange(half, dtype=F32) / half)
    ang = pos.astype(F32)[:, None] * inv
    cos, sin = jnp.cos(ang), jnp.sin(ang)
    reps = LANES // MLA_ROPE
    return (jnp.tile(jnp.concatenate([cos, cos], axis=1), (1, reps)),
            jnp.tile(jnp.concatenate([-sin, sin], axis=1), (1, reps)))


def _layer_weights(l, norm_gain, w_in, w_out, pool_w, pool_scale, mla_q_norm, mla_kv_norm, mla_w_uq, mla_w_uk,
                   mla_w_uv, conv_w, conv_b, nsa_cmp_pos_k, nsa_cmp_pos_v):
    w_p = _win_layout(w_in, l, tr=256)
    uq = mla_w_uq[l].reshape(MLA_Q_LORA, MLA_HEADS, MLA_NOPE + MLA_ROPE)
    return {
        "norm": norm_gain[l][None], "w_in": w_p, "w_out": w_out[l].astype(BF16),
        "pool_w": pool_w[l].astype(BF16), "pool_scale": pool_scale[l][None],
        "q_norm": mla_q_norm[l][None], "kv_norm": mla_kv_norm[l][None],
        "w_uq_nope": uq[:, :, :MLA_NOPE].reshape(MLA_Q_LORA, -1).astype(BF16),
        "w_uq_rope": uq[:, :, MLA_NOPE:].reshape(MLA_Q_LORA, -1).astype(BF16),
        "w_ukT": jnp.transpose(mla_w_uk[l], (1, 2, 0)).astype(BF16),
        "w_uv": jnp.transpose(mla_w_uv[l], (1, 0, 2)).astype(BF16),
        "conv_w": conv_w[l], "conv_b": conv_b[l][None],
        "gain_k": nsa_cmp_pos_k[l], "gain_v": nsa_cmp_pos_v[l],
    }


def _pad_rows(a, rows):
    return jnp.pad(a, ((0, 0), (0, rows - a.shape[1]), (0, 0)))


def kernel(x_prompt, x_sample, cache_mla_latent, cache_mla_krope, cache_nsa_cmp_k, cache_nsa_cmp_v,
           cache_nsa_sel_k, cache_nsa_sel_v, state_nsa_win_k, state_nsa_win_v, state_conv, state_pool,
           page_table, norm_gain, w_in, w_out, pool_w, pool_scale, mla_q_norm, mla_kv_norm,
           mla_w_uq, mla_w_uk, mla_w_uv, conv_w, conv_b, nsa_cmp_pos_k, nsa_cmp_pos_v, final_norm):
    batch, seq, _ = x_prompt.shape
    dec, dec_seq, _ = x_sample.shape
    depth = norm_gain.shape[0]
    n_pages = page_table.shape[1]
    n_past = n_pages * PAGE_SIZE
    wb = state_nsa_win_k.shape[2]
    assert dec_seq == 1 and seq % K_BLOCK == 0 and dec % 8 == 0
    n_sel = min(NSA_TOPN, -(-(n_past + dec_seq) // NSA_L_SEL))
    cp = 16 if n_pages % 32 == 0 else (8 if n_pages % 8 == 0 else n_pages)
    cache_krope_t = jnp.swapaxes(cache_mla_krope, 2, 3)

    cos_p, sin_p = _rope_tables(jnp.arange(seq, dtype=jnp.int32))
    cos_s, sin_s = _rope_tables(jnp.full((dec,), n_past, jnp.int32))
    final_gain = final_norm[None]

    xp = x_prompt.reshape(batch * seq, D_MODEL)
    xs = x_sample.reshape(dec, D_MODEL)
    new_p, new_s = [], []
    yp = ys = None
    for l in range(depth):
        final = l == depth - 1
        lw = _layer_weights(l, norm_gain, w_in, w_out, pool_w, pool_scale, mla_q_norm, mla_kv_norm, mla_w_uq,
                            mla_w_uk, mla_w_uv, conv_w, conv_b, nsa_cmp_pos_k, nsa_cmp_pos_v)
        zp = _inproj(xp, lw["norm"], lw["w_in"], tm=1024, tn=1536)
        qcat, kcat, lat, kr, kvb, kc, vc = _prep(zp, cos_p, sin_p, lw, tb=256, rows_per_table=seq, with_nsa=True)
        olat = _mla_prompt(qcat, kcat, batch=batch, seq=seq)
        ynsa = _nsa_prompt(zp, kc, vc, kvb, batch=batch, seq=seq)
        outs = _out_prompt(xp, zp, olat, ynsa, lw, final_gain, seq=seq, tb=256, final=final)
        if final:
            xp, yp, zc_tail = outs
        else:
            xp, zc_tail = outs
        z3 = zp.reshape(batch, seq, Z_WIDTH)
        wbp = min(NSA_WINDOW, seq)
        zc_last = zc_tail.reshape(batch, seq // 256, 8, 512)[:, -1]
        new_p.append((lat.reshape(batch, seq, LANES), kr.reshape(batch, seq, LANES)[:, :, :MLA_ROPE],
                      z3[:, :, Z_KC:Z_KC + NSA_DK], z3[:, :, Z_VC:Z_VC + NSA_DK],
                      z3[:, :, Z_KS:Z_KS + NSA_DK], z3[:, :, Z_VS:Z_VS + NSA_DK],
                      z3[:, seq - wbp:, Z_KW:Z_KW + NSA_DK], z3[:, seq - wbp:, Z_VW:Z_VW + NSA_DK],
                      zc_last[:, 8 - (CONV_WIDTH - 1):], z3[:, seq - POOL_HIST:, Z_POOL_U:Z_POOL_U + D_GROUP]))
        zs = _inproj(xs, lw["norm"], lw["w_in"], tm=dec, tn=1536)
        qcat_s, _, lat_s, kr_s = _prep(zs, cos_s, sin_s, lw, tb=dec, rows_per_table=dec, with_nsa=False)
        q4 = qcat_s.reshape(dec, MLA_HEADS, 256)
        qm = _pad_rows(q4[:, :, :LANES], HEAD_ROWS)
        qr = _pad_rows(q4[:, :, LANES:], HEAD_ROWS)
        qn = _pad_rows((zs[:, Z_NSA_Q:Z_NSA_Q + D_GROUP] * NSA_SCALE).astype(BF16).reshape(dec, NSA_HEADS, NSA_DK),
                       HEAD_ROWS)
        new_row = lambda a: _pad_rows(a[:, None, :], HEAD_ROWS)
        olat_s, ocmp_s, imp = _sample_paged(
            page_table, cache_mla_latent, cache_krope_t, cache_nsa_cmp_k, cache_nsa_cmp_v,
            qm, qr, qn, new_row(lat_s), new_row(kr_s), lw["gain_k"], lw["gain_v"], layer=l, cp=cp)
        blk = _select(imp.reshape(dec, -1), n_past=n_past, n_sel=n_sel)[:, :n_sel]
        page = jnp.minimum(blk // (PAGE_SIZE // NSA_L_SEL), n_pages - 1)
        phys = jnp.take_along_axis(page_table, page, axis=1)
        half = blk % (PAGE_SIZE // NSA_L_SEL)
        zcol = lambda c, w=NSA_DK: zs[:, c:c + w]
        ynsa_s, win_k_new, win_v_new = _sample_nsa(
            phys, half, blk, cache_nsa_sel_k, cache_nsa_sel_v, qn, new_row(zcol(Z_KS)), new_row(zcol(Z_VS)),
            state_nsa_win_k, state_nsa_win_v, new_row(zcol(Z_KW)), new_row(zcol(Z_VW)),
            zcol(Z_MISC, LANES)[:, None, :], ocmp_s, layer=l, n_past=n_past)
        outs = _out_sample(xs, zs, jnp.swapaxes(state_pool[l], 0, 1), state_conv[l, :, 0], state_conv[l, :, 1],
                           olat_s.reshape(dec, 512), ynsa_s.reshape(dec, 512), lw, final_gain,
                           n_past=n_past, final=final)
        if final:
            xs, ys, zc_s = outs
        else:
            xs, zc_s = outs
        app = lambda old, new: jnp.concatenate([old, new[:, None, :]], axis=1)[:, -old.shape[1]:]
        new_s.append((lat_s[:, None, :], kr_s[:, None, :MLA_ROPE],
                      zcol(Z_KC)[:, None], zcol(Z_VC)[:, None], zcol(Z_KS)[:, None], zcol(Z_VS)[:, None],
                      win_k_new, win_v_new,
                      app(state_conv[l], zc_s), app(state_pool[l], zcol(Z_POOL_U, D_GROUP))))
    stack = lambda rows: tuple(jnp.stack([r[i] for r in rows]) for i in range(len(rows[0])))
    return ((yp.reshape(batch, seq, D_MODEL), ys.reshape(dec, 1, D_MODEL)) + stack(new_p) + stack(new_s))
```
